```python
import jax, jax.numpy as jnp
from jax import lax
import numpy as np

D_MODEL = 1024
BATCH = 8
SEQ = 2048
DEPTH = 2
DEC_BATCH = 128
DEC_SEQ = 4
PAST_LEN = 16384
PAGE_SIZE = 128

N_META = 16
D_MIX = D_MODEL
D_MLSTM = D_MIX // 2
D_LRU = D_MIX - D_MLSTM
MLSTM_HEADS = 4
MLSTM_HEAD_DIM = D_MLSTM // MLSTM_HEADS
LRU_BLOCKS = 8
LRU_BLOCK_DIM = D_LRU // LRU_BLOCKS
CONV_WIDTH = 4
LRU_C = 8.0
CHUNK = 64
D_FF = ((8 * D_MODEL // 3 + 127) // 128) * 128
EPS = 1e-6
SPLITS = [D_MLSTM, 2 * D_MLSTM, 3 * D_MLSTM, 4 * D_MLSTM, 4 * D_MLSTM + MLSTM_HEADS,
          4 * D_MLSTM + 2 * MLSTM_HEADS, 4 * D_MLSTM + 2 * MLSTM_HEADS + D_LRU]
D_IN = 4 * D_MLSTM + 2 * MLSTM_HEADS + 2 * D_LRU

kernel_name = "hymba_mlstm_rglru_macaron_step"


def rmsnorm(x, g):
    xf = x.astype(jnp.float32)
    y = xf * lax.rsqrt(jnp.mean(xf * xf, axis=-1, keepdims=True) + EPS) * g.astype(jnp.float32)
    return y.astype(x.dtype)


def swiglu(x, w_gate, w_up, w_down):
    return (jax.nn.silu(x @ w_gate) * (x @ w_up)) @ w_down


def mlstm_chunk(q, k, v, ig, logf, C0, n0, m0):
    L = q.shape[2]
    b = jnp.cumsum(logf, axis=-1)
    causal = jnp.tril(jnp.ones((L, L), dtype=bool))
    D = jnp.where(causal, b[..., :, None] - b[..., None, :] + ig[..., None, :], -jnp.inf)
    inter = b + m0[..., None]
    m = jnp.maximum(jnp.max(D, axis=-1), inter)
    w_intra = jnp.exp(D - m[..., None])
    w_inter = jnp.exp(inter - m)
    s = jnp.einsum('bhtd,bhsd->bhts', q, k) * w_intra
    num = jnp.einsum('bhts,bhse->bhte', s, v) + w_inter[..., None] * jnp.einsum('bhed,bhtd->bhte', C0, q)
    den = jnp.sum(s, axis=-1) + w_inter * jnp.einsum('bhd,bhtd->bht', n0, q)
    h = num / jnp.maximum(jnp.abs(den), jnp.exp(-m))[..., None]
    m_last = m[..., -1]
    w_state = jnp.exp(b[..., -1:] - b + ig - m_last[..., None])
    decay = jnp.exp(inter[..., -1] - m_last)
    C = decay[..., None, None] * C0 + jnp.einsum('bhs,bhse,bhsd->bhed', w_state, v, k)
    n = decay[..., None] * n0 + jnp.einsum('bhs,bhsd->bhd', w_state, k)
    return h, (C, n, m_last)


def mlstm_prompt(q, k, v, ig, logf):
    B, H, T, dh = q.shape
    C0 = jnp.zeros((B, H, dh, dh), jnp.float32)
    n0 = jnp.zeros((B, H, dh), jnp.float32)
    m0 = jnp.zeros((B, H), jnp.float32)
    h_meta, state = mlstm_chunk(q[:, :, :N_META], k[:, :, :N_META], v[:, :, :N_META],
                                ig[:, :, :N_META], logf[:, :, :N_META], C0, n0, m0)
    n_chunks = (T - N_META) // CHUNK

    def blocks(a):
        r = a[:, :, N_META:]
        return jnp.moveaxis(r.reshape(r.shape[:2] + (n_chunks, CHUNK) + r.shape[3:]), 2, 0)

    def step(carry, xs):
        h, carry = mlstm_chunk(*xs, *carry)
        return carry, h

    state, hs = lax.scan(step, state, (blocks(q), blocks(k), blocks(v), blocks(ig), blocks(logf)))
    hs = jnp.moveaxis(hs, 0, 2).reshape(B, H, T - N_META, dh)
    return jnp.concatenate([h_meta, hs], axis=2), state


def linear_scan_combine(left, right):
    a_l, b_l = left
    a_r, b_r = right
    return a_l * a_r, a_r * b_l + b_r


def mixer(u, W, l, st):
    f32 = jnp.float32
    B, T, _ = u.shape
    H, dh = MLSTM_HEADS, MLSTM_HEAD_DIM
    z = u @ W['w_in'][l]
    q, k, v, o, ig, fg, xr, yg = jnp.split(z, SPLITS, axis=-1)

    def heads(a):
        return a.astype(f32).reshape(B, T, H, dh).transpose(0, 2, 1, 3)

    q = heads(q)
    k = heads(k) * (dh ** -0.5)
    v = heads(v)
    gates = jnp.concatenate([ig, fg], axis=-1).astype(f32) + W['b_gates'][l].astype(f32)
    ig = gates[..., :H].transpose(0, 2, 1)
    logf = jax.nn.log_sigmoid(gates[..., H:]).transpose(0, 2, 1)
    if st is None:
        h, (C, n, m) = mlstm_prompt(q, k, v, ig, logf)
        conv_buf = jnp.zeros((B, CONV_WIDTH - 1, D_LRU), f32)
        h0 = jnp.zeros((B, D_LRU), f32)
    else:
        C0, n0, m0, h0, conv_buf = (s.astype(f32) for s in st)
        h, (C, n, m) = mlstm_chunk(q, k, v, ig, logf, C0, n0, m0)
    h = h.transpose(0, 2, 1, 3)
    h = h * lax.rsqrt(jnp.mean(h * h, axis=-1, keepdims=True) + EPS) * W['g_mlstm_out'][l].astype(f32).reshape(H, dh)
    h_m = (jax.nn.sigmoid(o.astype(f32)).reshape(B, T, H, dh) * h).reshape(B, T, D_MLSTM)

    cw = W['conv_w'][l].astype(f32)
    xp = jnp.concatenate([conv_buf, xr.astype(f32)], axis=1)
    xc = W['conv_b'][l].astype(f32) + sum(xp[:, j:j + T] * cw[j] for j in range(CONV_WIDTH))
    new_buf = xp[:, T:]
    xb = xc.reshape(B, T, LRU_BLOCKS, LRU_BLOCK_DIM)
    r = jax.nn.sigmoid(jnp.einsum('btnd,nde->btne', xb, W['w_rg_a'][l].astype(f32)).reshape(B, T, D_LRU)
                       + W['b_rg_a'][l].astype(f32))
    i = jax.nn.sigmoid(jnp.einsum('btnd,nde->btne', xb, W['w_rg_x'][l].astype(f32)).reshape(B, T, D_LRU)
                       + W['b_rg_x'][l].astype(f32))
    log_a = -LRU_C * r * jax.nn.softplus(-W['lru_lambda'][l].astype(f32))
    a = jnp.exp(log_a)
    b_in = jnp.sqrt(-jnp.expm1(2.0 * log_a)) * (i * xc)
    A, Hc = lax.associative_scan(linear_scan_combine, (a, b_in), axis=1)
    h_l = Hc + A * h0[:, None]
    new_h = h_l[:, -1]
    h_l = h_l * lax.rsqrt(jnp.mean(h_l * h_l, axis=-1, keepdims=True) + EPS) * W['g_lru_out'][l].astype(f32)
    y_l = h_l * jax.nn.gelu(yg.astype(f32))

    mix = jnp.concatenate([h_m, y_l], axis=-1).astype(u.dtype) @ W['w_out'][l]
    return mix, (C, n, m, new_h, new_buf)


def trunk(x, W, states):
    collected = ([], [], [], [], [])
    for l in range(DEPTH):
        x = x + 0.5 * swiglu(rmsnorm(x, W['g_ff1'][l]), W['w_ff1_gate'][l], W['w_ff1_up'][l], W['w_ff1_down'][l])
        st = None if states is None else tuple(s[l] for s in states)
        mix, new_st = mixer(rmsnorm(x, W['g_mix'][l]), W, l, st)
        x = x + mix
        x = x + 0.5 * swiglu(rmsnorm(x, W['g_ff2'][l]), W['w_ff2_gate'][l], W['w_ff2_up'][l], W['w_ff2_down'][l])
        for lst, s in zip(collected, new_st):
            lst.append(s.astype(x.dtype))
    y = rmsnorm(x, W['g_final'])
    return y, tuple(jnp.stack(lst) for lst in collected)


def setup_inputs(seed: int = 0) -> dict:
    key = jax.random.key(seed)
    ks = jax.random.split(key, 40)
    f32 = jnp.float32

    def nrm(k, shape, scale):
        return jax.random.normal(k, shape, f32) * scale

    H, dh = MLSTM_HEADS, MLSTM_HEAD_DIM
    a_c = jax.random.uniform(ks[0], (DEPTH, D_LRU), f32, minval=0.9, maxval=0.999)
    s = a_c ** (1.0 / LRU_C)
    lru_lambda = jnp.log(s) - jnp.log1p(-s)
    b_gates = jnp.concatenate([nrm(ks[1], (DEPTH, H), 0.1),
                               3.0 + 3.0 * jax.random.uniform(ks[2], (DEPTH, H), f32)], axis=-1)
    return {
        'x_prompt': nrm(ks[3], (BATCH, SEQ, D_MODEL), 1.0),
        'x_sample': nrm(ks[4], (DEC_BATCH, DEC_SEQ, D_MODEL), 1.0),
        'state_mlstm_C': nrm(ks[5], (DEPTH, DEC_BATCH, H, dh, dh), 0.1),
        'state_mlstm_n': nrm(ks[6], (DEPTH, DEC_BATCH, H, dh), 0.5),
        'state_mlstm_m': nrm(ks[7], (DEPTH, DEC_BATCH, H), 1.0),
        'state_lru_h': nrm(ks[8], (DEPTH, DEC_BATCH, D_LRU), 0.5),
        'state_conv': nrm(ks[9], (DEPTH, DEC_BATCH, CONV_WIDTH - 1, D_LRU), 1.0),
        'meta_tokens': nrm(ks[10], (N_META, D_MODEL), 1.0),
        'g_ff1': 1.0 + nrm(ks[11], (DEPTH, D_MODEL), 0.02),
        'w_ff1_gate': nrm(ks[12], (DEPTH, D_MODEL, D_FF), D_MODEL ** -0.5),
        'w_ff1_up': nrm(ks[13], (DEPTH, D_MODEL, D_FF), D_MODEL ** -0.5),
        'w_ff1_down': nrm(ks[14], (DEPTH, D_FF, D_MODEL), D_FF ** -0.5),
        'g_mix': 1.0 + nrm(ks[15], (DEPTH, D_MODEL), 0.02),
        'w_in': nrm(ks[16], (DEPTH, D_MODEL, D_IN), D_MODEL ** -0.5),
        'b_gates': b_gates,
        'conv_w': nrm(ks[17], (DEPTH, CONV_WIDTH, D_LRU), CONV_WIDTH ** -0.5),
        'conv_b': nrm(ks[18], (DEPTH, D_LRU), 0.02),
        'w_rg_a': nrm(ks[19], (DEPTH, LRU_BLOCKS, LRU_BLOCK_DIM, LRU_BLOCK_DIM), LRU_BLOCK_DIM ** -0.5),
        'b_rg_a': nrm(ks[20], (DEPTH, D_LRU), 0.02),
        'w_rg_x': nrm(ks[21], (DEPTH, LRU_BLOCKS, LRU_BLOCK_DIM, LRU_BLOCK_DIM), LRU_BLOCK_DIM ** -0.5),
        'b_rg_x': nrm(ks[22], (DEPTH, D_LRU), 0.02),
        'lru_lambda': lru_lambda,
        'g_mlstm_out': 1.0 + nrm(ks[23], (DEPTH, D_MLSTM), 0.02),
        'g_lru_out': 1.0 + nrm(ks[24], (DEPTH, D_LRU), 0.02),
        'w_out': nrm(ks[25], (DEPTH, D_MIX, D_MODEL), D_MIX ** -0.5),
        'g_ff2': 1.0 + nrm(ks[26], (DEPTH, D_MODEL), 0.02),
        'w_ff2_gate': nrm(ks[27], (DEPTH, D_MODEL, D_FF), D_MODEL ** -0.5),
        'w_ff2_up': nrm(ks[28], (DEPTH, D_MODEL, D_FF), D_MODEL ** -0.5),
        'w_ff2_down': nrm(ks[29], (DEPTH, D_FF, D_MODEL), D_FF ** -0.5),
        'g_final': 1.0 + nrm(ks[30], (D_MODEL,), 0.02),
    }


def reference(x_prompt, x_sample, state_mlstm_C, state_mlstm_n, state_mlstm_m, state_lru_h, state_conv,
              meta_tokens, g_ff1, w_ff1_gate, w_ff1_up, w_ff1_down, g_mix, w_in, b_gates, conv_w, conv_b,
              w_rg_a, b_rg_a, w_rg_x, b_rg_x, lru_lambda, g_mlstm_out, g_lru_out, w_out,
              g_ff2, w_ff2_gate, w_ff2_up, w_ff2_down, g_final):
    W = dict(g_ff1=g_ff1, w_ff1_gate=w_ff1_gate, w_ff1_up=w_ff1_up, w_ff1_down=w_ff1_down,
             g_mix=g_mix, w_in=w_in, b_gates=b_gates, conv_w=conv_w, conv_b=conv_b,
             w_rg_a=w_rg_a, b_rg_a=b_rg_a, w_rg_x=w_rg_x, b_rg_x=b_rg_x, lru_lambda=lru_lambda,
             g_mlstm_out=g_mlstm_out, g_lru_out=g_lru_out, w_out=w_out,
             g_ff2=g_ff2, w_ff2_gate=w_ff2_gate, w_ff2_up=w_ff2_up, w_ff2_down=w_ff2_down, g_final=g_final)
    B = x_prompt.shape[0]
    meta = jnp.broadcast_to(meta_tokens.astype(x_prompt.dtype)[None], (B, N_META, D_MODEL))
    xp = jnp.concatenate([meta, x_prompt], axis=1)
    yp, (p_C, p_n, p_m, p_h, p_conv) = trunk(xp, W, None)
    y_prompt = yp[:, N_META:]
    y_sample, (s_C, s_n, s_m, s_h, s_conv) = trunk(
        x_sample, W, (state_mlstm_C, state_mlstm_n, state_mlstm_m, state_lru_h, state_conv))
    return (y_prompt, y_sample, p_C, p_n, p_m, p_h, p_conv, s_C, s_n, s_m, s_h, s_conv)
```

```python
import functools

import jax
import jax.numpy as jnp
from jax import lax
from jax.experimental import pallas as pl
from jax.experimental.pallas import tpu as pltpu

F32 = jnp.float32
BF16 = jnp.bfloat16

D_MODEL = 1024
D_FF = 2816
N_HEADS = 4
D_HEAD = 128
D_MLSTM = N_HEADS * D_HEAD
D_LRU = 512
N_META = 16
CONV_WIDTH = 4
LRU_C = 8.0
EPS = 1e-6

COL_Q, COL_K, COL_V, COL_O, COL_XR, COL_YG, COL_G = (i * 512 for i in range(7))
GATE_LANES = 128
D_INP = COL_G + GATE_LANES

TOKEN_TILE = 896
FF_CHUNK = 256
PROMPT_CHUNK = 256
META_BLOCK = 128
SAMPLE_SEQ_BLOCK = 8
VMEM_LIMIT_BYTES = 56 * 1024 * 1024

NT_DIMS = (((1,), (1,)), ((), ()))
TN_DIMS = (((0,), (0,)), ((), ()))


def _dot(a, b):
    return jnp.dot(a, b, preferred_element_type=F32)


def _rmsnorm(x, g):
    return x * lax.rsqrt(jnp.mean(x * x, axis=-1, keepdims=True) + EPS) * g


def _resident(shape):
    return pl.BlockSpec(shape, lambda *_: (0,) * len(shape), pipeline_mode=pl.Buffered(1))


def _ffn_kernel(*refs, first, mix, final, n_tiles):
    it = iter(refs)
    x_ref = next(it)
    tail_ref = next(it) if first else None
    if mix:
        hm_ref, yl_ref, wo_ref = next(it), next(it), next(it)
    g_ref, wg_ref, wu_ref, wd_ref = next(it), next(it), next(it), next(it)
    gf_ref = next(it) if final else None
    o_ref = next(it)
    xn_ref, acc_ref = next(it), next(it)

    x = x_ref[...]
    if first:
        x = jnp.where(pl.program_id(0) == n_tiles - 1, tail_ref[...], x)
    if mix:
        x = x + _dot(hm_ref[...], wo_ref[0:D_MLSTM, :]) + _dot(yl_ref[...], wo_ref[D_MLSTM:, :])
    xn_ref[...] = _rmsnorm(x, g_ref[...]).astype(BF16)
    for j in range(D_FF // FF_CHUNK):
        cols = slice(j * FF_CHUNK, (j + 1) * FF_CHUNK)
        gate = _dot(xn_ref[...], wg_ref[:, cols])
        up = _dot(xn_ref[...], wu_ref[:, cols])
        act = (gate * jax.nn.sigmoid(gate) * up).astype(BF16)
        down = _dot(act, wd_ref[cols, :])
        if j == 0:
            acc_ref[...] = down
        else:
            acc_ref[...] += down
    y = x + 0.5 * acc_ref[...]
    if final:
        y = _rmsnorm(y, gf_ref[...])
    o_ref[...] = y


def _ffn(x, g, wg, wu, wd, *, tail=None, mix=None, g_final=None, n_rows):
    n_tiles = n_rows // TOKEN_TILE
    first = tail is not None
    row_spec = lambda width: pl.BlockSpec((TOKEN_TILE, width), lambda i: (i, 0))
    args, specs = [x], []
    if first:
        specs.append(pl.BlockSpec((TOKEN_TILE, D_MODEL), lambda i: (jnp.minimum(i, n_tiles - 2), 0)))
        args.append(tail)
        specs.append(_resident((TOKEN_TILE, D_MODEL)))
    else:
        specs.append(row_spec(D_MODEL))
    if mix is not None:
        hm, yl, wo = mix
        args += [hm, yl, wo]
        specs += [row_spec(D_MLSTM), row_spec(D_LRU), _resident(wo.shape)]
    args += [g, wg, wu, wd]
    specs += [_resident(g.shape), _resident(wg.shape), _resident(wu.shape), _resident(wd.shape)]
    if g_final is not None:
        args.append(g_final)
        specs.append(_resident(g_final.shape))
    return pl.pallas_call(
        functools.partial(_ffn_kernel, first=first, mix=mix is not None, final=g_final is not None,
                          n_tiles=n_tiles),
        grid=(n_tiles,),
        in_specs=specs,
        out_specs=row_spec(D_MODEL),
        out_shape=jax.ShapeDtypeStruct((n_rows, D_MODEL), F32),
        scratch_shapes=[pltpu.VMEM((TOKEN_TILE, D_MODEL), BF16), pltpu.VMEM((TOKEN_TILE, D_MODEL), F32)],
        compiler_params=pltpu.CompilerParams(dimension_semantics=("arbitrary",),
                                             vmem_limit_bytes=VMEM_LIMIT_BYTES),
        name="ffn",
    )(*args)


def _inproj_kernel(x_ref, g_ref, w_ref, z_ref, xn_ref):
    xn_ref[...] = _rmsnorm(x_ref[...], g_ref[...]).astype(BF16)
    for c0 in range(0, D_INP, 512):
        cols = slice(c0, min(c0 + 512, D_INP))
        z_ref[:, cols] = _dot(xn_ref[...], w_ref[:, cols])


def _inproj(x, g, w):
    n_rows = x.shape[0]
    return pl.pallas_call(
        _inproj_kernel,
        grid=(n_rows // TOKEN_TILE,),
        in_specs=[pl.BlockSpec((TOKEN_TILE, D_MODEL), lambda i: (i, 0)), _resident(g.shape), _resident(w.shape)],
        out_specs=pl.BlockSpec((TOKEN_TILE, D_INP), lambda i: (i, 0)),
        out_shape=jax.ShapeDtypeStruct((n_rows, D_INP), F32),
        scratch_shapes=[pltpu.VMEM((TOKEN_TILE, D_MODEL), BF16)],
        compiler_params=pltpu.CompilerParams(dimension_semantics=("arbitrary",),
                                             vmem_limit_bytes=VMEM_LIMIT_BYTES),
        name="inproj",
    )(x, g, w)


def _scan_rows(x, op, identity):
    n = x.shape[0]
    row = lax.broadcasted_iota(jnp.int32, x.shape, 0)
    d = 1
    while d < n:
        x = op(x, jnp.where(row >= d, pltpu.roll(x, d, 0), identity))
        d *= 2
    return x


def _gate_lanes(g):
    return g, jax.nn.log_sigmoid(pltpu.roll(g, GATE_LANES - N_HEADS, 1))


def _head_out(h, gout, o):
    h = h * lax.rsqrt(jnp.mean(h * h, axis=-1, keepdims=True) + EPS) * gout
    return jax.nn.sigmoid(o) * h


def _mlstm_chunk_kernel(q_ref, k_ref, v_ref, o_ref, g_ref, bg_ref, gout_ref, c0_ref, n0_ref, m0_ref,
                        *rest, chunk, n_chunks, aliased):
    if aliased:
        rest = rest[1:]
    hm_ref, c_out_ref, n_out_ref, m_out_ref, c_s, n_s, m_s = rest
    step = pl.program_id(1)

    @pl.when(step == 0)
    def _():
        c_s[...] = c0_ref[0]
        n_s[0:N_HEADS, :] = n0_ref[0]
        m_s[0:1, :] = m0_ref[0]

    ig, logf = _gate_lanes(g_ref[0:chunk, :] + bg_ref[...])
    b = _scan_rows(logf, jnp.add, 0.0)
    c = ig - b
    m0 = m_s[0:1, :]
    big_m = jnp.maximum(_scan_rows(c, jnp.maximum, -jnp.inf), m0)
    m = b + big_m
    w_inter = jnp.exp(m0 - big_m)
    exp_neg_m = jnp.exp(-m)
    m_last = big_m[chunk - 1:chunk, :]
    w_state = jnp.exp(c - m_last)
    decay = jnp.exp(m0 - m_last)

    row = lax.broadcasted_iota(jnp.int32, (chunk, chunk), 0)
    col = lax.broadcasted_iota(jnp.int32, (chunk, chunk), 1)
    if hm_ref.shape[0] != chunk:
        hm_ref[...] = jnp.zeros(hm_ref.shape, hm_ref.dtype)
    for h in range(N_HEADS):
        lanes = slice(h * D_HEAD, (h + 1) * D_HEAD)
        one = slice(h, h + 1)
        qf = q_ref[0:chunk, lanes]
        kf = k_ref[0:chunk, lanes] * (D_HEAD ** -0.5)
        vf = v_ref[0:chunk, lanes]
        qb, kb = qf.astype(BF16), kf.astype(BF16)
        c_row = jnp.sum(jnp.where(row == col, c[:, one], 0.0), axis=0, keepdims=True)
        w_intra = jnp.exp(jnp.where(col <= row, c_row - big_m[:, one], -jnp.inf))
        s = lax.dot_general(qb, kb, NT_DIMS, preferred_element_type=F32) * w_intra
        c_prev = c_s[h]
        wi = w_inter[:, one]
        num = _dot(s.astype(BF16), vf.astype(BF16)) + wi * lax.dot_general(
            qb, c_prev.astype(BF16), NT_DIMS, preferred_element_type=F32)
        den = jnp.sum(s, axis=1, keepdims=True) + wi * jnp.sum(qf * n_s[one, :], axis=1, keepdims=True)
        hh = num / jnp.maximum(jnp.abs(den), exp_neg_m[:, one])
        hm_ref[0:chunk, lanes] = _head_out(hh, gout_ref[:, lanes], o_ref[0:chunk, lanes]).astype(BF16)
        ws = w_state[:, one]
        c_s[h] = decay[:, one] * c_prev + lax.dot_general(
            (ws * vf).astype(BF16), kb, TN_DIMS, preferred_element_type=F32)
        n_s[one, :] = decay[:, one] * n_s[one, :] + jnp.sum(ws * kf, axis=0, keepdims=True)
    m_s[0:1, :] = m[chunk - 1:chunk, :]

    @pl.when(step == n_chunks - 1)
    def _():
        c_out_ref[0] = c_s[...]
        n_out_ref[0] = n_s[0:N_HEADS, :]
        m_out_ref[0] = m_s[0:1, :]


def _mlstm_chunked(z, bg, gout, state, hm_prev, *, n_seq, seq_len, chunk, row0, n_rows):
    n_chunks = seq_len // chunk
    meta = hm_prev is None
    in_rows = chunk if not meta else chunk
    blk0 = row0 // in_rows
    zspec = lambda width, colblk: pl.BlockSpec(
        (in_rows, width), lambda b, c: (blk0 + b * n_chunks + c, colblk))
    c0, n0, m0 = state
    args = [z, z, z, z, z, bg, gout, c0, n0, m0]
    specs = [zspec(512, COL_Q // 512), zspec(512, COL_K // 512), zspec(512, COL_V // 512),
             zspec(512, COL_O // 512), zspec(GATE_LANES, COL_G // GATE_LANES),
             _resident(bg.shape), _resident(gout.shape),
             pl.BlockSpec((1, N_HEADS, D_HEAD, D_HEAD), lambda b, c: (0, 0, 0, 0)),
             pl.BlockSpec((1, N_HEADS, D_HEAD), lambda b, c: (0, 0, 0)),
             pl.BlockSpec((1, 1, GATE_LANES), lambda b, c: (0, 0, 0))]
    aliases = {}
    if meta:
        hm_spec = pl.BlockSpec((META_BLOCK, D_MLSTM), lambda b, c: (row0 // META_BLOCK, 0))
    else:
        args.append(hm_prev)
        specs.append(pl.BlockSpec(memory_space=pl.ANY))
        aliases = {len(args) - 1: 0}
        hm_spec = pl.BlockSpec((chunk, D_MLSTM), lambda b, c: (blk0 + b * n_chunks + c, 0))
    return pl.pallas_call(
        functools.partial(_mlstm_chunk_kernel, chunk=chunk, n_chunks=n_chunks, aliased=not meta),
        grid=(n_seq, n_chunks),
        in_specs=specs,
        out_specs=[hm_spec,
                   pl.BlockSpec((1, N_HEADS, D_HEAD, D_HEAD), lambda b, c: (b, 0, 0, 0)),
                   pl.BlockSpec((1, N_HEADS, D_HEAD), lambda b, c: (b, 0, 0)),
                   pl.BlockSpec((1, 1, GATE_LANES), lambda b, c: (b, 0, 0))],
        out_shape=[jax.ShapeDtypeStruct((n_rows, D_MLSTM), BF16),
                   jax.ShapeDtypeStruct((n_seq, N_HEADS, D_HEAD, D_HEAD), F32),
                   jax.ShapeDtypeStruct((n_seq, N_HEADS, D_HEAD), F32),
                   jax.ShapeDtypeStruct((n_seq, 1, GATE_LANES), F32)],
        scratch_shapes=[pltpu.VMEM((N_HEADS, D_HEAD, D_HEAD), F32), pltpu.VMEM((8, D_HEAD), F32),
                        pltpu.VMEM((8, GATE_LANES), F32)],
        input_output_aliases=aliases,
        compiler_params=pltpu.CompilerParams(dimension_semantics=("arbitrary", "arbitrary"),
                                             vmem_limit_bytes=VMEM_LIMIT_BYTES),
        name="mlstm_meta" if meta else "mlstm_prompt",
    )(*args)


def _mlstm_step_kernel(*refs, n_seq, n_tok):
    nh = N_HEADS
    q_refs, k_refs, v_refs, o_refs = refs[0:nh], refs[nh:2 * nh], refs[2 * nh:3 * nh], refs[3 * nh:4 * nh]
    (g_ref, bg_ref, gout_ref, c0_ref, n0_ref, m0_ref, _hm_prev,
     hm_ref, c_out_ref, n_out_ref, m_out_ref, inter_s, wv_s, hm_s) = refs[4 * nh:]

    def token(ref, t):
        return ref[pl.ds(t, n_seq, stride=n_tok), :]

    m0 = m0_ref[...]
    b, c, big_m = [], [], []
    for t in range(n_tok):
        ig, logf = _gate_lanes(token(g_ref, t) + bg_ref[...])
        b.append(logf if t == 0 else b[-1] + logf)
        c.append(ig - b[-1])
        big_m.append(jnp.maximum(c[-1], m0 if t == 0 else big_m[-1]))
    m_last = big_m[-1]
    decay = jnp.exp(m0 - m_last)
    scale = D_HEAD ** -0.5

    for j in range(n_seq):
        rows = slice(j * n_tok, (j + 1) * n_tok)
        for h in range(nh):
            inter_s[h, rows, :] = lax.dot_general(
                q_refs[h][rows, :].astype(BF16), c0_ref[j, h].astype(BF16), NT_DIMS, preferred_element_type=F32)

    for h in range(nh):
        lanes = slice(h * D_HEAD, (h + 1) * D_HEAD)
        one = slice(h, h + 1)
        q = [token(q_refs[h], t) for t in range(n_tok)]
        k = [token(k_refs[h], t) * scale for t in range(n_tok)]
        v = [token(v_refs[h], t) for t in range(n_tok)]
        n0 = n0_ref[:, lanes]
        for t in range(n_tok):
            mt = big_m[t][:, one]
            wi = jnp.exp(m0[:, one] - mt)
            num = wi * inter_s[h, pl.ds(t, n_seq, stride=n_tok), :]
            den = wi * jnp.sum(q[t] * n0, axis=1, keepdims=True)
            for s in range(t + 1):
                w = jnp.sum(q[t] * k[s], axis=1, keepdims=True) * jnp.exp(c[s][:, one] - mt)
                num = num + w * v[s]
                den = den + w
            hh = num / jnp.maximum(jnp.abs(den), jnp.exp(-(b[t][:, one] + mt)))
            hm_s[h, pl.ds(t, n_seq, stride=n_tok), :] = _head_out(hh, gout_ref[:, lanes], token(o_refs[h], t))
        n_new = decay[:, one] * n0
        for s in range(n_tok):
            ws = jnp.exp(c[s][:, one] - m_last[:, one])
            wv_s[h, pl.ds(s, n_seq, stride=n_tok), :] = ws * v[s]
            n_new = n_new + ws * k[s]
        n_out_ref[:, lanes] = n_new
        hm_ref[:, lanes] = hm_s[h].astype(BF16)
    m_out_ref[...] = b[-1] + m_last

    for j in range(n_seq):
        rows = slice(j * n_tok, (j + 1) * n_tok)
        for h in range(nh):
            kb = (k_refs[h][rows, :] * scale).astype(BF16)
            c_out_ref[j, h] = decay[j:j + 1, h:h + 1] * c0_ref[j, h] + lax.dot_general(
                wv_s[h, rows, :].astype(BF16), kb, TN_DIMS, preferred_element_type=F32)


def _mlstm_step(z, bg, gout, state, hm_prev, *, n_seq, n_tok, row0):
    sb = SAMPLE_SEQ_BLOCK
    rows = sb * n_tok
    blk0 = row0 // rows
    head_specs = lambda col0: [pl.BlockSpec((rows, D_HEAD), functools.partial(
        lambda i, cb: (blk0 + i, cb), cb=col0 // D_HEAD + h)) for h in range(N_HEADS)]
    c0, n0, m0 = state
    n_z = 4 * N_HEADS + 1
    return pl.pallas_call(
        functools.partial(_mlstm_step_kernel, n_seq=sb, n_tok=n_tok),
        grid=(n_seq // sb,),
        in_specs=head_specs(COL_Q) + head_specs(COL_K) + head_specs(COL_V) + head_specs(COL_O) + [
            pl.BlockSpec((rows, GATE_LANES), lambda i: (blk0 + i, COL_G // GATE_LANES)),
            _resident(bg.shape), _resident(gout.shape),
            pl.BlockSpec((sb, N_HEADS, D_HEAD, D_HEAD), lambda i: (i, 0, 0, 0)),
            pl.BlockSpec((sb, D_MLSTM), lambda i: (i, 0)),
            pl.BlockSpec((sb, GATE_LANES), lambda i: (i, 0)),
            pl.BlockSpec(memory_space=pl.ANY)],
        out_specs=[pl.BlockSpec((rows, D_MLSTM), lambda i: (blk0 + i, 0)),
                   pl.BlockSpec((sb, N_HEADS, D_HEAD, D_HEAD), lambda i: (i, 0, 0, 0)),
                   pl.BlockSpec((sb, D_MLSTM), lambda i: (i, 0)),
                   pl.BlockSpec((sb, GATE_LANES), lambda i: (i, 0))],
        out_shape=[jax.ShapeDtypeStruct(hm_prev.shape, BF16),
                   jax.ShapeDtypeStruct(c0.shape, F32),
                   jax.ShapeDtypeStruct(n0.shape, F32),
                   jax.ShapeDtypeStruct(m0.shape, F32)],
        scratch_shapes=[pltpu.VMEM((N_HEADS, rows, D_HEAD), F32)] * 3,
        input_output_aliases={n_z + 5: 0},
        compiler_params=pltpu.CompilerParams(dimension_semantics=("arbitrary",),
                                             vmem_limit_bytes=VMEM_LIMIT_BYTES),
        name="mlstm_step",
    )(*([z] * n_z), bg, gout, c0, n0, m0, hm_prev)


def _expm1(y):
    return jnp.tanh(0.5 * y) * (jnp.exp(y) + 1.0)


def _lru_gates(xc, wa_ref, ba_ref, wx_ref, bx_ref, lam_ref):
    xb = xc.astype(BF16)
    r = jax.nn.sigmoid(_dot(xb, wa_ref[...]) + ba_ref[...])
    i = jax.nn.sigmoid(_dot(xb, wx_ref[...]) + bx_ref[...])
    log_a = -LRU_C * r * jax.nn.softplus(-lam_ref[...])
    return jnp.exp(log_a), jnp.sqrt(-_expm1(2.0 * log_a)) * (i * xc)


def _lru_out(h, g, yg):
    h = h * lax.rsqrt(jnp.mean(h * h, axis=-1, keepdims=True) + EPS) * g
    return h * jax.nn.gelu(yg)


def _lru_chunk_kernel(xr_ref, yg_ref, cw_ref, cb_ref, wa_ref, ba_ref, wx_ref, bx_ref, lam_ref, gl_ref,
                      h0_ref, buf0_ref, *rest, chunk, n_chunks, aliased):
    if aliased:
        rest = rest[1:]
    yl_ref, h_out_ref, buf_out_ref, ext_s, h_s = rest
    step = pl.program_id(1)
    taps = CONV_WIDTH - 1

    @pl.when(step == 0)
    def _():
        ext_s[8 - taps:8, :] = buf0_ref[0]
        h_s[0:1, :] = h0_ref[0]

    ext_s[8:8 + chunk, :] = xr_ref[0:chunk, :]
    xc = cb_ref[...]
    for j in range(CONV_WIDTH):
        xc = xc + ext_s[8 - taps + j:8 - taps + j + chunk, :] * cw_ref[j:j + 1, :]
    a, bx = _lru_gates(xc, wa_ref, ba_ref, wx_ref, bx_ref, lam_ref)

    row = lax.broadcasted_iota(jnp.int32, a.shape, 0)
    d = 1
    while d < chunk:
        live = row >= d
        bx = a * jnp.where(live, pltpu.roll(bx, d, 0), 0.0) + bx
        a = a * jnp.where(live, pltpu.roll(a, d, 0), 1.0)
        d *= 2
    h = bx + a * h_s[0:1, :]

    if yl_ref.shape[0] != chunk:
        yl_ref[...] = jnp.zeros(yl_ref.shape, yl_ref.dtype)
    yl_ref[0:chunk, :] = _lru_out(h, gl_ref[...], yg_ref[0:chunk, :]).astype(BF16)
    h_s[0:1, :] = h[chunk - 1:chunk, :]
    ext_s[8 - taps:8, :] = ext_s[8 + chunk - taps:8 + chunk, :]

    @pl.when(step == n_chunks - 1)
    def _():
        h_out_ref[0] = h_s[0:1, :]
        buf_out_ref[0] = ext_s[8 - taps:8, :]


def _lru_chunked(z, lw, state, yl_prev, *, n_seq, seq_len, chunk, row0, n_rows):
    n_chunks = seq_len // chunk
    meta = yl_prev is None
    blk0 = row0 // chunk
    zspec = lambda colblk: pl.BlockSpec((chunk, D_LRU), lambda b, c: (blk0 + b * n_chunks + c, colblk))
    h0, buf0 = state
    args = [z, z, *lw, h0, buf0]
    specs = [zspec(COL_XR // D_LRU), zspec(COL_YG // D_LRU)] + [_resident(w.shape) for w in lw] + [
        pl.BlockSpec((1, 1, D_LRU), lambda b, c: (0, 0, 0)),
        pl.BlockSpec((1, CONV_WIDTH - 1, D_LRU), lambda b, c: (0, 0, 0))]
    aliases = {}
    if meta:
        yl_spec = pl.BlockSpec((META_BLOCK, D_LRU), lambda b, c: (row0 // META_BLOCK, 0))
    else:
        args.append(yl_prev)
        specs.append(pl.BlockSpec(memory_space=pl.ANY))
        aliases = {len(args) - 1: 0}
        yl_spec = pl.BlockSpec((chunk, D_LRU), lambda b, c: (blk0 + b * n_chunks + c, 0))
    return pl.pallas_call(
        functools.partial(_lru_chunk_kernel, chunk=chunk, n_chunks=n_chunks, aliased=not meta),
        grid=(n_seq, n_chunks),
        in_specs=specs,
        out_specs=[yl_spec,
                   pl.BlockSpec((1, 1, D_LRU), lambda b, c: (b, 0, 0)),
                   pl.BlockSpec((1, CONV_WIDTH - 1, D_LRU), lambda b, c: (b, 0, 0))],
        out_shape=[jax.ShapeDtypeStruct((n_rows, D_LRU), BF16),
                   jax.ShapeDtypeStruct((n_seq, 1, D_LRU), F32),
                   jax.ShapeDtypeStruct((n_seq, CONV_WIDTH - 1, D_LRU), F32)],
        scratch_shapes=[pltpu.VMEM((8 + chunk, D_LRU), F32), pltpu.VMEM((8, D_LRU), F32)],
        input_output_aliases=aliases,
        compiler_params=pltpu.CompilerParams(dimension_semantics=("arbitrary", "arbitrary"),
                                             vmem_limit_bytes=VMEM_LIMIT_BYTES),
        name="lru_meta" if meta else "lru_prompt",
    )(*args)


def _lru_step_kernel(*refs, n_seq, n_tok):
    nb = D_LRU // 128
    xr_refs, yg_refs = refs[0:nb], refs[nb:2 * nb]
    (cw_ref, cb_ref, wa_ref, ba_ref, wx_ref, bx_ref, lam_ref, gl_ref, h0_ref, buf0_ref, _yl_prev,
     yl_ref, h_out_ref, buf_out_ref, xc_s, yg_s, yl_s) = refs[2 * nb:]
    taps = CONV_WIDTH - 1
    for cblk in range(nb):
        lanes = slice(cblk * 128, (cblk + 1) * 128)
        xp = [buf0_ref[:, j * D_LRU + cblk * 128:j * D_LRU + (cblk + 1) * 128] for j in range(taps)]
        xp += [xr_refs[cblk][pl.ds(t, n_seq, stride=n_tok), :] for t in range(n_tok)]
        for t in range(n_tok):
            xc = cb_ref[:, lanes]
            for j in range(CONV_WIDTH):
                xc = xc + xp[t + j] * cw_ref[j:j + 1, lanes]
            xc_s[t * n_seq:(t + 1) * n_seq, lanes] = xc
            yg_s[t * n_seq:(t + 1) * n_seq, lanes] = yg_refs[cblk][pl.ds(t, n_seq, stride=n_tok), :]
        for j in range(taps):
            buf_out_ref[:, j * D_LRU + cblk * 128:j * D_LRU + (cblk + 1) * 128] = xp[n_tok + j]
    a, bx = _lru_gates(xc_s[...], wa_ref, ba_ref, wx_ref, bx_ref, lam_ref)
    h = h0_ref[...]
    for t in range(n_tok):
        rows = slice(t * n_seq, (t + 1) * n_seq)
        h = a[rows] * h + bx[rows]
        y = _lru_out(h, gl_ref[...], yg_s[rows, :])
        for cblk in range(nb):
            yl_s[cblk, pl.ds(t, n_seq, stride=n_tok), :] = y[:, cblk * 128:(cblk + 1) * 128]
    for cblk in range(nb):
        yl_ref[:, cblk * 128:(cblk + 1) * 128] = yl_s[cblk].astype(BF16)
    h_out_ref[...] = h


def _lru_step(z, lw, state, yl_prev, *, n_seq, n_tok, row0):
    rows = n_seq * n_tok
    blk0 = row0 // rows
    h0, buf0 = state
    whole = lambda a: pl.BlockSpec(a.shape, lambda i: (0,) * a.ndim)
    nb = D_LRU // 128
    lane_specs = lambda col0: [pl.BlockSpec((rows, 128), functools.partial(
        lambda i, cb: (blk0, cb), cb=col0 // 128 + c)) for c in range(nb)]
    n_z = 2 * nb
    return pl.pallas_call(
        functools.partial(_lru_step_kernel, n_seq=n_seq, n_tok=n_tok),
        grid=(1,),
        in_specs=lane_specs(COL_XR) + lane_specs(COL_YG) + [whole(w) for w in lw] + [
            whole(h0), whole(buf0), pl.BlockSpec(memory_space=pl.ANY)],
        out_specs=[pl.BlockSpec((rows, D_LRU), lambda i: (blk0, 0)), whole(h0), whole(buf0)],
        out_shape=[jax.ShapeDtypeStruct(yl_prev.shape, BF16),
                   jax.ShapeDtypeStruct(h0.shape, F32),
                   jax.ShapeDtypeStruct(buf0.shape, F32)],
        scratch_shapes=[pltpu.VMEM((rows, D_LRU), F32), pltpu.VMEM((rows, D_LRU), F32),
                        pltpu.VMEM((nb, rows, 128), F32)],
        input_output_aliases={n_z + len(lw) + 2: 0},
        compiler_params=pltpu.CompilerParams(dimension_semantics=("arbitrary",),
                                             vmem_limit_bytes=VMEM_LIMIT_BYTES),
        name="lru_step",
    )(*([z] * n_z), *lw, h0, buf0, yl_prev)


def _block_diag(w):
    n, d, _ = w.shape
    eye = jnp.eye(n, dtype=w.dtype)
    return (eye[:, None, :, None] * w[:, :, None, :]).reshape(n * d, n * d)


def kernel(x_prompt, x_sample, state_mlstm_C, state_mlstm_n, state_mlstm_m, state_lru_h, state_conv, meta_tokens, g_ff1, w_ff1_gate, w_ff1_up, w_ff1_down, g_mix, w_in, b_gates, conv_w, conv_b, w_rg_a, b_rg_a, w_rg_x, b_rg_x, lru_lambda, g_mlstm_out, g_lru_out, w_out, g_ff2, w_ff2_gate, w_ff2_up, w_ff2_down, g_final):
    batch, seq, _ = x_prompt.shape
    n_samp, n_tok, _ = x_sample.shape
    depth = w_in.shape[0]
    n_prompt = batch * seq
    samp0 = n_prompt
    meta0 = samp0 + n_samp * n_tok
    n_rows = meta0 + META_BLOCK
    assert n_rows % TOKEN_TILE == 0 and n_prompt >= n_rows - TOKEN_TILE
    assert seq % PROMPT_CHUNK == 0 and meta0 % META_BLOCK == 0 and n_samp % SAMPLE_SEQ_BLOCK == 0
    taps = CONV_WIDTH - 1

    xp_flat = x_prompt.reshape(n_prompt, D_MODEL)
    tail = jnp.concatenate([
        xp_flat[n_rows - TOKEN_TILE:],
        x_sample.reshape(n_samp * n_tok, D_MODEL),
        meta_tokens.astype(F32),
        jnp.zeros((META_BLOCK - N_META, D_MODEL), F32)], axis=0)

    row = lambda a: a.reshape(1, -1)
    gate_cols = w_in[:, :, 4 * D_MLSTM:4 * D_MLSTM + 2 * N_HEADS]
    w_in_p = jnp.concatenate([
        w_in[:, :, :4 * D_MLSTM], w_in[:, :, 4 * D_MLSTM + 2 * N_HEADS:], gate_cols,
        jnp.zeros((depth, D_MODEL, GATE_LANES - 2 * N_HEADS), w_in.dtype)], axis=-1).astype(BF16)
    bg_p = jnp.pad(b_gates.astype(F32), ((0, 0), (0, GATE_LANES - 2 * N_HEADS)))

    zero_state = (jnp.zeros((1, N_HEADS, D_HEAD, D_HEAD), F32), jnp.zeros((1, N_HEADS, D_HEAD), F32),
                  jnp.zeros((1, 1, GATE_LANES), F32))
    zero_lru = (jnp.zeros((1, 1, D_LRU), F32), jnp.zeros((1, taps, D_LRU), F32))

    outs = {k: [] for k in ("pC", "pn", "pm", "ph", "pb", "sC", "sn", "sm", "sh", "sb")}
    x = None
    for l in range(depth):
        ff1 = (row(g_ff1[l]), w_ff1_gate[l].astype(BF16), w_ff1_up[l].astype(BF16), w_ff1_down[l].astype(BF16))
        ff2 = (row(g_ff2[l]), w_ff2_gate[l].astype(BF16), w_ff2_up[l].astype(BF16), w_ff2_down[l].astype(BF16))
        if l == 0:
            x = _ffn(xp_flat, *ff1, tail=tail, n_rows=n_rows)
        else:
            x = _ffn(x, *ff1, n_rows=n_rows)
        z = _inproj(x, row(g_mix[l]), w_in_p[l])

        bg, gout = row(bg_p[l]), row(g_mlstm_out[l].astype(F32))
        hm, mC, mn, mm = _mlstm_chunked(z, bg, gout, zero_state, None, n_seq=1, seq_len=N_META,
                                        chunk=N_META, row0=meta0, n_rows=n_rows)
        hm, pC, pn, pm = _mlstm_chunked(z, bg, gout, (mC, mn, mm), hm, n_seq=batch, seq_len=seq,
                                        chunk=PROMPT_CHUNK, row0=0, n_rows=n_rows)
        s_state = (state_mlstm_C[l].astype(F32), state_mlstm_n[l].astype(F32).reshape(n_samp, D_MLSTM),
                   jnp.pad(state_mlstm_m[l].astype(F32), ((0, 0), (0, GATE_LANES - N_HEADS))))
        hm, sC, sn, sm = _mlstm_step(z, bg, gout, s_state, hm, n_seq=n_samp, n_tok=n_tok, row0=samp0)

        lw = (conv_w[l].astype(F32), row(conv_b[l].astype(F32)),
              _block_diag(w_rg_a[l]).astype(BF16), row(b_rg_a[l].astype(F32)),
              _block_diag(w_rg_x[l]).astype(BF16), row(b_rg_x[l].astype(F32)),
              row(lru_lambda[l].astype(F32)), row(g_lru_out[l].astype(F32)))
        yl, mh, mb = _lru_chunked(z, lw, zero_lru, None, n_seq=1, seq_len=N_META, chunk=N_META,
                                  row0=meta0, n_rows=n_rows)
        yl, ph, pb = _lru_chunked(z, lw, (mh, mb), yl, n_seq=batch, seq_len=seq, chunk=PROMPT_CHUNK,
                                  row0=0, n_rows=n_rows)
        yl, sh, sb = _lru_step(z, lw, (state_lru_h[l].astype(F32),
                                       state_conv[l].astype(F32).reshape(n_samp, taps * D_LRU)),
                               yl, n_seq=n_samp, n_tok=n_tok, row0=samp0)

        x = _ffn(x, *ff2, mix=(hm, yl, w_out[l].astype(BF16)),
                 g_final=row(g_final.astype(F32)) if l == depth - 1 else None, n_rows=n_rows)

        for key, val in (("pC", pC), ("pn", pn), ("pm", pm[:, 0, :N_HEADS]), ("ph", ph[:, 0]), ("pb", pb),
                         ("sC", sC), ("sn", sn.reshape(n_samp, N_HEADS, D_HEAD)), ("sm", sm[:, :N_HEADS]),
                         ("sh", sh), ("sb", sb.reshape(n_samp, taps, D_LRU))):
            outs[key].append(val)

    y_prompt = x[:n_prompt].reshape(batch, seq, D_MODEL)
    y_sample = x[samp0:meta0].reshape(n_samp, n_tok, D_MODEL)
    st = {k: jnp.stack(v) for k, v in outs.items()}
    return (y_prompt, y_sample, st["pC"], st["pn"], st["pm"], st["ph"], st["pb"],
            st["sC"], st["sn"], st["sm"], st["sh"], st["sb"])
```

```python
import functools

import jax
import jax.numpy as jnp
from jax import lax
from jax.experimental import pallas as pl
from jax.experimental.pallas import tpu as pltpu

F32 = jnp.float32
BF16 = jnp.bfloat16

D_MODEL = 1024
D_FF = 2816
N_HEADS = 4
D_HEAD = 128
D_MLSTM = N_HEADS * D_HEAD
D_LRU = 512
N_META = 16
CONV_WIDTH = 4
LRU_C = 8.0
EPS = 1e-6

COL_Q, COL_K, COL_V, COL_O, COL_XR, COL_YG, COL_G = (i * 512 for i in range(7))
GATE_LANES = 128
D_INP = COL_G + GATE_LANES

SUBLANES = 8
MAIN_TILE = 1024
FF_CHUNK = 256
PROMPT_CHUNK = 256
SAMPLE_SEQ_BLOCK = 8
VMEM_LIMIT_BYTES = 56 * 1024 * 1024

NT_DIMS = (((1,), (1,)), ((), ()))
TN_DIMS = (((0,), (0,)), ((), ()))


def _dot(a, b):
    return jnp.dot(a, b, preferred_element_type=F32)


def _sigmoid(x):
    return 0.5 * jnp.tanh(0.5 * x) + 0.5


def _rmsnorm(x, g):
    return x * lax.rsqrt(jnp.mean(x * x, axis=-1, keepdims=True) + EPS) * g


def _resident(shape):
    return pl.BlockSpec(shape, lambda *_: (0,) * len(shape), pipeline_mode=pl.Buffered(1))


def _layer(arr, l):
    return pl.BlockSpec((None,) + arr.shape[1:], lambda *_: (l, 0, 0), pipeline_mode=pl.Buffered(1))


def _params(dimension_semantics):
    return pltpu.CompilerParams(dimension_semantics=dimension_semantics, vmem_limit_bytes=VMEM_LIMIT_BYTES)


def _token_call(body, main_in, tail_in, consts, const_specs, out_widths, scratch_shapes, name):
    n_main = main_in[0].shape[0] // MAIN_TILE
    n_tail = tail_in[0].shape[0]
    n_row, n_const, n_out = len(main_in), len(consts), len(out_widths)

    def kernel(*refs):
        main_refs, tail_refs = refs[:n_row], refs[n_row:2 * n_row]
        const_refs = refs[2 * n_row:2 * n_row + n_const]
        outs = refs[2 * n_row + n_const:]
        main_outs, tail_outs, scratch = outs[:n_out], outs[n_out:2 * n_out], outs[2 * n_out:]
        step = pl.program_id(0)

        @pl.when(step == 0)
        def _():
            body(tail_refs, const_refs, tail_outs, scratch, n_tail)

        @pl.when(step > 0)
        def _():
            body(main_refs, const_refs, main_outs, scratch, MAIN_TILE)

    main_spec = lambda a_w: pl.BlockSpec((MAIN_TILE, a_w), lambda i: (jnp.maximum(i - 1, 0), 0))
    tail_spec = lambda a_w: pl.BlockSpec((n_tail, a_w), lambda i: (0, 0))
    return pl.pallas_call(
        kernel,
        grid=(n_main + 1,),
        in_specs=[main_spec(a.shape[1]) for a in main_in]
        + [pl.BlockSpec((n_tail, a.shape[1]), lambda i: (0, 0), pipeline_mode=pl.Buffered(1)) for a in tail_in]
        + list(const_specs),
        out_specs=[main_spec(w) for w in out_widths] + [tail_spec(w) for w in out_widths],
        out_shape=[jax.ShapeDtypeStruct((n_main * MAIN_TILE, w), F32) for w in out_widths]
        + [jax.ShapeDtypeStruct((n_tail, w), F32) for w in out_widths],
        scratch_shapes=scratch_shapes,
        compiler_params=_params(("arbitrary",)),
        name=name,
    )(*main_in, *tail_in, *consts)


def _ffn_body(row_refs, const_refs, out_refs, scratch, rows, *, mix, final):
    it = iter(const_refs)
    if mix:
        x_ref, hm_ref, yl_ref = row_refs
        wo_ref = next(it)
    else:
        (x_ref,) = row_refs
    g_ref, wg_ref, wu_ref, wd_ref = next(it), next(it), next(it), next(it)
    (o_ref,) = out_refs
    xn_ref = scratch[0].at[0:rows]

    x = x_ref[...]
    if mix:
        x = x + _dot(hm_ref[...], wo_ref[0:D_MLSTM, :]) + _dot(yl_ref[...], wo_ref[D_MLSTM:, :])
    xn_ref[...] = _rmsnorm(x, g_ref[...]).astype(BF16)
    o_ref[...] = x
    for j in range(D_FF // FF_CHUNK):
        cols = slice(j * FF_CHUNK, (j + 1) * FF_CHUNK)
        gate = _dot(xn_ref[...], wg_ref[:, cols])
        up = _dot(xn_ref[...], wu_ref[:, cols])
        act = (gate * _sigmoid(gate) * up).astype(BF16)
        o_ref[...] += 0.5 * _dot(act, wd_ref[cols, :])
    if final:
        o_ref[...] = _rmsnorm(o_ref[...], next(it)[...])


def _ffn(x, g, wg, wu, wd, l, *, mix=None, g_final=None):
    x_main, x_tail = x
    main_in, tail_in, consts, specs = [x_main], [x_tail], [], []
    if mix is not None:
        (hm_main, hm_tail), (yl_main, yl_tail), wo = mix
        main_in += [hm_main, yl_main]
        tail_in += [hm_tail, yl_tail]
        consts.append(wo)
        specs.append(_layer(wo, l))
    consts += [g, wg, wu, wd]
    specs += [_layer(g, l), _layer(wg, l), _layer(wu, l), _layer(wd, l)]
    if g_final is not None:
        consts.append(g_final)
        specs.append(_resident(g_final.shape))
    body = functools.partial(_ffn_body, mix=mix is not None, final=g_final is not None)
    return _token_call(body, main_in, tail_in, consts, specs, [D_MODEL],
                       [pltpu.VMEM((MAIN_TILE, D_MODEL), BF16)], "ffn")


def _inproj_body(row_refs, const_refs, out_refs, scratch, rows):
    (x_ref,) = row_refs
    g_ref, wa_ref, wb_ref, wg_ref = const_refs
    (z_ref,) = out_refs
    xn_ref = scratch[0].at[0:rows]
    xn_ref[...] = _rmsnorm(x_ref[...], g_ref[...]).astype(BF16)
    for c0 in range(0, COL_XR, 512):
        z_ref[:, c0:c0 + 512] = _dot(xn_ref[...], wa_ref[:, c0:c0 + 512])
    for c0 in range(0, COL_G - COL_XR, 512):
        z_ref[:, COL_XR + c0:COL_XR + c0 + 512] = _dot(xn_ref[...], wb_ref[:, c0:c0 + 512])
    z_ref[:, COL_G:] = _dot(xn_ref[...], wg_ref[...])


def _inproj(x, g, w_a, w_b, w_g, l):
    x_main, x_tail = x
    consts = [g, w_a, w_b, w_g]
    return _token_call(_inproj_body, [x_main], [x_tail], consts, [_layer(c, l) for c in consts], [D_INP],
                       [pltpu.VMEM((MAIN_TILE, D_MODEL), BF16)], "inproj")


def _scan_rows(x, op, identity):
    n = x.shape[0]
    row = lax.broadcasted_iota(jnp.int32, x.shape, 0)
    d = 1
    while d < n:
        x = op(x, jnp.where(row >= d, pltpu.roll(x, d, 0), identity))
        d *= 2
    return x


def _gate_lanes(g):
    return g, jax.nn.log_sigmoid(pltpu.roll(g, GATE_LANES - N_HEADS, 1))


def _head_out(h, gout, o):
    h = h * lax.rsqrt(jnp.mean(h * h, axis=-1, keepdims=True) + EPS) * gout
    return _sigmoid(o) * h


def _mlstm_chunk_kernel(q_ref, k_ref, v_ref, o_ref, g_ref, bg_ref, gout_ref, c0_ref, n0_ref, m0_ref,
                        hm_ref, c_out_ref, n_out_ref, m_out_ref, c_s, n_s, m_s, *, chunk, n_chunks):
    step = pl.program_id(1)

    @pl.when(step == 0)
    def _():
        c_s[...] = c0_ref[0]
        n_s[0:N_HEADS, :] = n0_ref[0]
        m_s[0:1, :] = m0_ref[0]

    ig, logf = _gate_lanes(g_ref[...] + bg_ref[...])
    b = _scan_rows(logf, jnp.add, 0.0)
    c = ig - b
    m0 = m_s[0:1, :]
    big_m = jnp.maximum(_scan_rows(c, jnp.maximum, -jnp.inf), m0)
    m = b + big_m
    w_inter = jnp.exp(m0 - big_m)
    exp_neg_m = jnp.exp(-m)
    m_last = big_m[chunk - 1:chunk, :]
    w_state = jnp.exp(c - m_last)
    decay = jnp.exp(m0 - m_last)

    row = lax.broadcasted_iota(jnp.int32, (chunk, chunk), 0)
    col = lax.broadcasted_iota(jnp.int32, (chunk, chunk), 1)
    for h in range(N_HEADS):
        lanes = slice(h * D_HEAD, (h + 1) * D_HEAD)
        one = slice(h, h + 1)
        qf = q_ref[:, lanes]
        kf = k_ref[:, lanes] * (D_HEAD ** -0.5)
        vf = v_ref[:, lanes]
        qb, kb = qf.astype(BF16), kf.astype(BF16)
        c_row = jnp.sum(jnp.where(row == col, c[:, one], 0.0), axis=0, keepdims=True)
        w_intra = jnp.exp(jnp.where(col <= row, c_row - big_m[:, one], -jnp.inf))
        s = lax.dot_general(qb, kb, NT_DIMS, preferred_element_type=F32) * w_intra
        c_prev = c_s[h]
        wi = w_inter[:, one]
        num = _dot(s.astype(BF16), vf.astype(BF16)) + wi * lax.dot_general(
            qb, c_prev.astype(BF16), NT_DIMS, preferred_element_type=F32)
        den = jnp.sum(s, axis=1, keepdims=True) + wi * jnp.sum(qf * n_s[one, :], axis=1, keepdims=True)
        hh = num / jnp.maximum(jnp.abs(den), exp_neg_m[:, one])
        hm_ref[:, lanes] = _head_out(hh, gout_ref[:, lanes], o_ref[:, lanes]).astype(BF16)
        ws = w_state[:, one]
        c_s[h] = decay[:, one] * c_prev + lax.dot_general(
            (ws * vf).astype(BF16), kb, TN_DIMS, preferred_element_type=F32)
        n_s[one, :] = decay[:, one] * n_s[one, :] + jnp.sum(ws * kf, axis=0, keepdims=True)
    m_s[0:1, :] = m[chunk - 1:chunk, :]

    @pl.when(step == n_chunks - 1)
    def _():
        c_out_ref[0] = c_s[...]
        n_out_ref[0] = n_s[0:N_HEADS, :]
        m_out_ref[0] = m_s[0:1, :]


def _mlstm_chunked(z, bg, gout, l, state, *, n_seq, seq_len, chunk, row0, name):
    n_chunks = seq_len // chunk
    blk0 = row0 // chunk
    zspec = lambda width, colblk: pl.BlockSpec((chunk, width), lambda b, c: (blk0 + b * n_chunks + c, colblk))
    c0, n0, m0 = state
    return pl.pallas_call(
        functools.partial(_mlstm_chunk_kernel, chunk=chunk, n_chunks=n_chunks),
        grid=(n_seq, n_chunks),
        in_specs=[zspec(512, COL_Q // 512), zspec(512, COL_K // 512), zspec(512, COL_V // 512),
                  zspec(512, COL_O // 512), zspec(GATE_LANES, COL_G // GATE_LANES),
                  _layer(bg, l), _layer(gout, l),
                  pl.BlockSpec((1, N_HEADS, D_HEAD, D_HEAD), lambda b, c: (0, 0, 0, 0)),
                  pl.BlockSpec((1, N_HEADS, D_HEAD), lambda b, c: (0, 0, 0)),
                  pl.BlockSpec((1, 1, GATE_LANES), lambda b, c: (0, 0, 0))],
        out_specs=[pl.BlockSpec((chunk, D_MLSTM), lambda b, c: (b * n_chunks + c, 0)),
                   pl.BlockSpec((1, N_HEADS, D_HEAD, D_HEAD), lambda b, c: (b, 0, 0, 0)),
                   pl.BlockSpec((1, N_HEADS, D_HEAD), lambda b, c: (b, 0, 0)),
                   pl.BlockSpec((1, 1, GATE_LANES), lambda b, c: (b, 0, 0))],
        out_shape=[jax.ShapeDtypeStruct((n_seq * seq_len, D_MLSTM), BF16),
                   jax.ShapeDtypeStruct((n_seq, N_HEADS, D_HEAD, D_HEAD), F32),
                   jax.ShapeDtypeStruct((n_seq, N_HEADS, D_HEAD), F32),
                   jax.ShapeDtypeStruct((n_seq, 1, GATE_LANES), F32)],
        scratch_shapes=[pltpu.VMEM((N_HEADS, D_HEAD, D_HEAD), F32), pltpu.VMEM((SUBLANES, D_HEAD), F32),
                        pltpu.VMEM((SUBLANES, GATE_LANES), F32)],
        compiler_params=_params(("arbitrary", "arbitrary")),
        name=name,
    )(z, z, z, z, z, bg, gout, c0, n0, m0)


def _mlstm_step_kernel(*refs, n_seq, n_tok, aliased):
    nh = N_HEADS
    q_refs, k_refs, v_refs, o_refs = refs[0:nh], refs[nh:2 * nh], refs[2 * nh:3 * nh], refs[3 * nh:4 * nh]
    g_ref, bg_ref, gout_ref, c0_ref, n0_ref, m0_ref = refs[4 * nh:4 * nh + 6]
    hm_ref, c_out_ref, n_out_ref, m_out_ref, inter_s, wv_s, hm_s = refs[4 * nh + 6 + int(aliased):]

    def token(ref, t):
        return ref[pl.ds(t, n_seq, stride=n_tok), :]

    m0 = m0_ref[...]
    b, c, big_m = [], [], []
    for t in range(n_tok):
        ig, logf = _gate_lanes(token(g_ref, t) + bg_ref[...])
        b.append(logf if t == 0 else b[-1] + logf)
        c.append(ig - b[-1])
        big_m.append(jnp.maximum(c[-1], m0 if t == 0 else big_m[-1]))
    m_last = big_m[-1]
    decay = jnp.exp(m0 - m_last)
    scale = D_HEAD ** -0.5

    for j in range(n_seq):
        rows = slice(j * n_tok, (j + 1) * n_tok)
        for h in range(nh):
            inter_s[h, rows, :] = lax.dot_general(
                q_refs[h][rows, :].astype(BF16), c0_ref[j, h].astype(BF16), NT_DIMS, preferred_element_type=F32)

    for h in range(nh):
        lanes = slice(h * D_HEAD, (h + 1) * D_HEAD)
        one = slice(h, h + 1)
        q = [token(q_refs[h], t) for t in range(n_tok)]
        k = [token(k_refs[h], t) * scale for t in range(n_tok)]
        v = [token(v_refs[h], t) for t in range(n_tok)]
        n0 = n0_ref[:, lanes]
        for t in range(n_tok):
            mt = big_m[t][:, one]
            wi = jnp.exp(m0[:, one] - mt)
            num = wi * inter_s[h, pl.ds(t, n_seq, stride=n_tok), :]
            den = wi * jnp.sum(q[t] * n0, axis=1, keepdims=True)
            for s in range(t + 1):
                w = jnp.sum(q[t] * k[s], axis=1, keepdims=True) * jnp.exp(c[s][:, one] - mt)
                num = num + w * v[s]
                den = den + w
            hh = num / jnp.maximum(jnp.abs(den), jnp.exp(-(b[t][:, one] + mt)))
            hm_s[h, pl.ds(t, n_seq, stride=n_tok), :] = _head_out(hh, gout_ref[:, lanes], token(o_refs[h], t))
        n_new = decay[:, one] * n0
        for s in range(n_tok):
            ws = jnp.exp(c[s][:, one] - m_last[:, one])
            wv_s[h, pl.ds(s, n_seq, stride=n_tok), :] = ws * v[s]
            n_new = n_new + ws * k[s]
        n_out_ref[:, lanes] = n_new
        hm_ref[:, lanes] = hm_s[h].astype(BF16)
    m_out_ref[...] = b[-1] + m_last

    for j in range(n_seq):
        rows = slice(j * n_tok, (j + 1) * n_tok)
        for h in range(nh):
            kb = (k_refs[h][rows, :] * scale).astype(BF16)
            c_out_ref[j, h] = decay[j:j + 1, h:h + 1] * c0_ref[j, h] + lax.dot_general(
                wv_s[h, rows, :].astype(BF16), kb, TN_DIMS, preferred_element_type=F32)


def _mlstm_step(z, bg, gout, l, c_all, n0, m0, c_out_prev, *, n_seq, n_tok):
    sb = SAMPLE_SEQ_BLOCK
    rows = sb * n_tok
    head_specs = lambda col0: [pl.BlockSpec((rows, D_HEAD), functools.partial(
        lambda i, cb: (i, cb), cb=col0 // D_HEAD + h)) for h in range(N_HEADS)]
    c_spec = pl.BlockSpec((None, sb, N_HEADS, D_HEAD, D_HEAD), lambda i: (l, i, 0, 0, 0))
    n_z = 4 * N_HEADS + 1
    args = [z] * n_z + [bg, gout, c_all, n0, m0]
    specs = head_specs(COL_Q) + head_specs(COL_K) + head_specs(COL_V) + head_specs(COL_O) + [
        pl.BlockSpec((rows, GATE_LANES), lambda i: (i, COL_G // GATE_LANES)),
        _layer(bg, l), _layer(gout, l), c_spec,
        pl.BlockSpec((sb, D_MLSTM), lambda i: (i, 0)),
        pl.BlockSpec((sb, GATE_LANES), lambda i: (i, 0))]
    aliases = {}
    if c_out_prev is not None:
        args.append(c_out_prev)
        specs.append(pl.BlockSpec(memory_space=pl.ANY))
        aliases = {len(args) - 1: 1}
    return pl.pallas_call(
        functools.partial(_mlstm_step_kernel, n_seq=sb, n_tok=n_tok, aliased=c_out_prev is not None),
        grid=(n_seq // sb,),
        in_specs=specs,
        out_specs=[pl.BlockSpec((rows, D_MLSTM), lambda i: (i, 0)), c_spec,
                   pl.BlockSpec((sb, D_MLSTM), lambda i: (i, 0)),
                   pl.BlockSpec((sb, GATE_LANES), lambda i: (i, 0))],
        out_shape=[jax.ShapeDtypeStruct((n_seq * n_tok, D_MLSTM), BF16),
                   jax.ShapeDtypeStruct(c_all.shape, F32),
                   jax.ShapeDtypeStruct(n0.shape, F32),
                   jax.ShapeDtypeStruct(m0.shape, F32)],
        scratch_shapes=[pltpu.VMEM((N_HEADS, rows, D_HEAD), F32)] * 3,
        input_output_aliases=aliases,
        compiler_params=_params(("arbitrary",)),
        name="mlstm_step",
    )(*args)


def _expm1(y):
    return jnp.tanh(0.5 * y) * (jnp.exp(y) + 1.0)


def _lru_gates(xc, wa_ref, ba_ref, wx_ref, bx_ref, lam_ref):
    xb = xc.astype(BF16)
    r = _sigmoid(_dot(xb, wa_ref[...]) + ba_ref[...])
    i = _sigmoid(_dot(xb, wx_ref[...]) + bx_ref[...])
    log_a = -LRU_C * r * jax.nn.softplus(-lam_ref[...])
    return jnp.exp(log_a), jnp.sqrt(-_expm1(2.0 * log_a)) * (i * xc)


def _lru_out(h, g, yg):
    h = h * lax.rsqrt(jnp.mean(h * h, axis=-1, keepdims=True) + EPS) * g
    return h * jax.nn.gelu(yg)


def _lru_chunk_kernel(xr_ref, yg_ref, cw_ref, cb_ref, wa_ref, ba_ref, wx_ref, bx_ref, lam_ref, gl_ref,
                      h0_ref, buf0_ref, yl_ref, h_out_ref, buf_out_ref, prev_s, carry_s, h_s, *, chunk, n_chunks):
    step = pl.program_id(1)
    taps = CONV_WIDTH - 1

    @pl.when(step == 0)
    def _():
        prev_s[...] = jnp.zeros(prev_s.shape, F32)
        prev_s[SUBLANES - taps:SUBLANES, :] = buf0_ref[0]
        carry_s[0:1, :] = h0_ref[0]

    ext = jnp.concatenate([prev_s[...], xr_ref[...]], axis=0)
    acc = ext * cw_ref[0:1, :]
    for j in range(1, CONV_WIDTH):
        acc = pltpu.roll(acc, 1, 0) + ext * cw_ref[j:j + 1, :]
    xc = acc[SUBLANES:, :] + cb_ref[...]
    prev_s[...] = ext[chunk:, :]
    a, bx = _lru_gates(xc, wa_ref, ba_ref, wx_ref, bx_ref, lam_ref)

    groups = chunk // SUBLANES
    a3 = a.reshape(groups, SUBLANES, D_LRU)
    b3 = bx.reshape(groups, SUBLANES, D_LRU)
    sub = lax.broadcasted_iota(jnp.int32, a3.shape, 1)
    d = 1
    while d < SUBLANES:
        live = sub >= d
        b3 = a3 * jnp.where(live, pltpu.roll(b3, d, 1), 0.0) + b3
        a3 = a3 * jnp.where(live, pltpu.roll(a3, d, 1), 1.0)
        d *= 2
    carry = carry_s[0:1, :]
    for g in range(groups):
        hg = b3[g] + a3[g] * carry
        h_s[g * SUBLANES:(g + 1) * SUBLANES, :] = hg
        carry = hg[SUBLANES - 1:SUBLANES, :]
    carry_s[0:1, :] = carry

    yl_ref[...] = _lru_out(h_s[...], gl_ref[...], yg_ref[...]).astype(BF16)

    @pl.when(step == n_chunks - 1)
    def _():
        h_out_ref[0] = carry_s[0:1, :]
        buf_out_ref[0] = prev_s[SUBLANES - taps:SUBLANES, :]


def _lru_chunked(z, lw, l, state, *, n_seq, seq_len, chunk, row0, name):
    n_chunks = seq_len // chunk
    blk0 = row0 // chunk
    zspec = lambda colblk: pl.BlockSpec((chunk, D_LRU), lambda b, c: (blk0 + b * n_chunks + c, colblk))
    h0, buf0 = state
    return pl.pallas_call(
        functools.partial(_lru_chunk_kernel, chunk=chunk, n_chunks=n_chunks),
        grid=(n_seq, n_chunks),
        in_specs=[zspec(COL_XR // D_LRU), zspec(COL_YG // D_LRU)] + [_layer(w, l) for w in lw] + [
            pl.BlockSpec((1, 1, D_LRU), lambda b, c: (0, 0, 0)),
            pl.BlockSpec((1, CONV_WIDTH - 1, D_LRU), lambda b, c: (0, 0, 0))],
        out_specs=[pl.BlockSpec((chunk, D_LRU), lambda b, c: (b * n_chunks + c, 0)),
                   pl.BlockSpec((1, 1, D_LRU), lambda b, c: (b, 0, 0)),
                   pl.BlockSpec((1, CONV_WIDTH - 1, D_LRU), lambda b, c: (b, 0, 0))],
        out_shape=[jax.ShapeDtypeStruct((n_seq * seq_len, D_LRU), BF16),
                   jax.ShapeDtypeStruct((n_seq, 1, D_LRU), F32),
                   jax.ShapeDtypeStruct((n_seq, CONV_WIDTH - 1, D_LRU), F32)],
        scratch_shapes=[pltpu.VMEM((SUBLANES, D_LRU), F32), pltpu.VMEM((SUBLANES, D_LRU), F32),
                        pltpu.VMEM((chunk, D_LRU), F32)],
        compiler_params=_params(("arbitrary", "arbitrary")),
        name=name,
    )(z, z, *lw, h0, buf0)


def _lru_step_kernel(*refs, n_seq, n_tok):
    nb = D_LRU // 128
    xr_refs, yg_refs = refs[0:nb], refs[nb:2 * nb]
    (cw_ref, cb_ref, wa_ref, ba_ref, wx_ref, bx_ref, lam_ref, gl_ref, h0_ref, buf0_ref,
     yl_ref, h_out_ref, buf_out_ref, xc_s, yg_s, yl_s) = refs[2 * nb:]
    taps = CONV_WIDTH - 1
    for cblk in range(nb):
        lanes = slice(cblk * 128, (cblk + 1) * 128)
        xp = [buf0_ref[:, j * D_LRU + cblk * 128:j * D_LRU + (cblk + 1) * 128] for j in range(taps)]
        xp += [xr_refs[cblk][pl.ds(t, n_seq, stride=n_tok), :] for t in range(n_tok)]
        for t in range(n_tok):
            xc = cb_ref[:, lanes]
            for j in range(CONV_WIDTH):
                xc = xc + xp[t + j] * cw_ref[j:j + 1, lanes]
            xc_s[t * n_seq:(t + 1) * n_seq, lanes] = xc
            yg_s[t * n_seq:(t + 1) * n_seq, lanes] = yg_refs[cblk][pl.ds(t, n_seq, stride=n_tok), :]
        for j in range(taps):
            buf_out_ref[:, j * D_LRU + cblk * 128:j * D_LRU + (cblk + 1) * 128] = xp[n_tok + j]
    a, bx = _lru_gates(xc_s[...], wa_ref, ba_ref, wx_ref, bx_ref, lam_ref)
    h = h0_ref[...]
    for t in range(n_tok):
        rows = slice(t * n_seq, (t + 1) * n_seq)
        h = a[rows] * h + bx[rows]
        y = _lru_out(h, gl_ref[...], yg_s[rows, :])
        for cblk in range(nb):
            yl_s[cblk, pl.ds(t, n_seq, stride=n_tok), :] = y[:, cblk * 128:(cblk + 1) * 128]
    for cblk in range(nb):
        yl_ref[:, cblk * 128:(cblk + 1) * 128] = yl_s[cblk].astype(BF16)
    h_out_ref[...] = h


def _lru_step(z, lw, l, state, *, n_seq, n_tok):
    rows = n_seq * n_tok
    nb = D_LRU // 128
    h0, buf0 = state
    whole = lambda a: pl.BlockSpec(a.shape, lambda i: (0,) * a.ndim)
    lane_specs = lambda col0: [pl.BlockSpec((rows, 128), functools.partial(
        lambda i, cb: (0, cb), cb=col0 // 128 + c)) for c in range(nb)]
    return pl.pallas_call(
        functools.partial(_lru_step_kernel, n_seq=n_seq, n_tok=n_tok),
        grid=(1,),
        in_specs=lane_specs(COL_XR) + lane_specs(COL_YG) + [_layer(w, l) for w in lw] + [whole(h0), whole(buf0)],
        out_specs=[pl.BlockSpec((rows, D_LRU), lambda i: (0, 0)), whole(h0), whole(buf0)],
        out_shape=[jax.ShapeDtypeStruct((rows, D_LRU), BF16),
                   jax.ShapeDtypeStruct(h0.shape, F32),
                   jax.ShapeDtypeStruct(buf0.shape, F32)],
        scratch_shapes=[pltpu.VMEM((rows, D_LRU), F32), pltpu.VMEM((rows, D_LRU), F32),
                        pltpu.VMEM((nb, rows, 128), F32)],
        compiler_params=_params(("arbitrary",)),
        name="lru_step",
    )(*([z] * (2 * nb)), *lw, h0, buf0)


def _block_diag(w):
    depth, n, d, _ = w.shape
    eye = jnp.eye(n, dtype=w.dtype)
    return (eye[None, :, None, :, None] * w[:, :, :, None, :]).reshape(depth, n * d, n * d)


def kernel(x_prompt, x_sample, state_mlstm_C, state_mlstm_n, state_mlstm_m, state_lru_h, state_conv, meta_tokens, g_ff1, w_ff1_gate, w_ff1_up, w_ff1_down, g_mix, w_in, b_gates, conv_w, conv_b, w_rg_a, b_rg_a, w_rg_x, b_rg_x, lru_lambda, g_mlstm_out, g_lru_out, w_out, g_ff2, w_ff2_gate, w_ff2_up, w_ff2_down, g_final):
    batch, seq, _ = x_prompt.shape
    n_samp, n_tok, _ = x_sample.shape
    depth = w_in.shape[0]
    n_prompt = batch * seq
    n_dec = n_samp * n_tok
    assert n_prompt % MAIN_TILE == 0 and seq % PROMPT_CHUNK == 0
    assert n_dec % N_META == 0 and n_samp % SAMPLE_SEQ_BLOCK == 0
    taps = CONV_WIDTH - 1

    x = (x_prompt.reshape(n_prompt, D_MODEL),
         jnp.concatenate([x_sample.reshape(n_dec, D_MODEL), meta_tokens.astype(F32)], axis=0))

    vec = lambda a: a.astype(F32).reshape(depth, 1, -1)
    ff1 = (vec(g_ff1), w_ff1_gate.astype(BF16), w_ff1_up.astype(BF16), w_ff1_down.astype(BF16))
    ff2 = (vec(g_ff2), w_ff2_gate.astype(BF16), w_ff2_up.astype(BF16), w_ff2_down.astype(BF16))
    n_gate = 2 * N_HEADS
    w_a = w_in[:, :, :4 * D_MLSTM].astype(BF16)
    w_b = w_in[:, :, 4 * D_MLSTM + n_gate:].astype(BF16)
    w_g = jnp.pad(w_in[:, :, 4 * D_MLSTM:4 * D_MLSTM + n_gate], ((0, 0), (0, 0), (0, GATE_LANES - n_gate))).astype(BF16)
    bg = jnp.pad(b_gates.astype(F32), ((0, 0), (0, GATE_LANES - n_gate))).reshape(depth, 1, GATE_LANES)
    gout = vec(g_mlstm_out)
    wo = w_out.astype(BF16)
    lw = (conv_w.astype(F32), vec(conv_b), _block_diag(w_rg_a).astype(BF16), vec(b_rg_a),
          _block_diag(w_rg_x).astype(BF16), vec(b_rg_x), vec(lru_lambda), vec(g_lru_out))
    g_fin = g_final.astype(F32).reshape(1, D_MODEL)

    zero_state = (jnp.zeros((1, N_HEADS, D_HEAD, D_HEAD), F32), jnp.zeros((1, N_HEADS, D_HEAD), F32),
                  jnp.zeros((1, 1, GATE_LANES), F32))
    zero_lru = (jnp.zeros((1, 1, D_LRU), F32), jnp.zeros((1, taps, D_LRU), F32))
    c_all = state_mlstm_C.astype(F32)
    n_all = state_mlstm_n.astype(F32).reshape(depth, n_samp, D_MLSTM)
    m_all = jnp.pad(state_mlstm_m.astype(F32), ((0, 0), (0, 0), (0, GATE_LANES - N_HEADS)))
    h_all = state_lru_h.astype(F32)
    buf_all = state_conv.astype(F32).reshape(depth, n_samp, taps * D_LRU)

    outs = {k: [] for k in ("pC", "pn", "pm", "ph", "pb", "sn", "sm", "sh", "sb")}
    s_c = None
    for l in range(depth):
        x = _ffn(x, *ff1, l)
        z_main, z_tail = _inproj(x, vec(g_mix), w_a, w_b, w_g, l)

        hm_meta, m_c, m_n, m_m = _mlstm_chunked(z_tail, bg, gout, l, zero_state, n_seq=1, seq_len=N_META,
                                                chunk=N_META, row0=n_dec, name="mlstm_meta")
        hm_main, p_c, p_n, p_m = _mlstm_chunked(z_main, bg, gout, l, (m_c, m_n, m_m), n_seq=batch, seq_len=seq,
                                                chunk=PROMPT_CHUNK, row0=0, name="mlstm_prompt")
        hm_dec, s_c, s_n, s_m = _mlstm_step(z_tail, bg, gout, l, c_all, n_all[l], m_all[l], s_c,
                                            n_seq=n_samp, n_tok=n_tok)

        yl_meta, m_h, m_b = _lru_chunked(z_tail, lw, l, zero_lru, n_seq=1, seq_len=N_META, chunk=N_META,
                                         row0=n_dec, name="lru_meta")
        yl_main, p_h, p_b = _lru_chunked(z_main, lw, l, (m_h, m_b), n_seq=batch, seq_len=seq,
                                         chunk=PROMPT_CHUNK, row0=0, name="lru_prompt")
        yl_dec, s_h, s_b = _lru_step(z_tail, lw, l, (h_all[l], buf_all[l]), n_seq=n_samp, n_tok=n_tok)

        mix = ((hm_main, jnp.concatenate([hm_dec, hm_meta], axis=0)),
               (yl_main, jnp.concatenate([yl_dec, yl_meta], axis=0)), wo)
        x = _ffn(x, *ff2, l, mix=mix, g_final=g_fin if l == depth - 1 else None)

        for key, val in (("pC", p_c), ("pn", p_n), ("pm", p_m[:, 0, :N_HEADS]), ("ph", p_h[:, 0]), ("pb", p_b),
                         ("sn", s_n.reshape(n_samp, N_HEADS, D_HEAD)), ("sm", s_m[:, :N_HEADS]),
                         ("sh", s_h), ("sb", s_b.reshape(n_samp, taps, D_LRU))):
            outs[key].append(val)

    y_main, y_tail = x
    st = {k: jnp.stack(v) for k, v in outs.items()}
    return (y_main.reshape(batch, seq, D_MODEL), y_tail[:n_dec].reshape(n_samp, n_tok, D_MODEL),
            st["pC"], st["pn"], st["pm"], st["ph"], st["pb"], s_c, st["sn"], st["sm"], st["sh"], st["sb"])
```

```python
import functools

import jax
import jax.numpy as jnp
from jax import lax
from jax.experimental import pallas as pl
from jax.experimental.pallas import tpu as pltpu

F32 = jnp.float32
BF16 = jnp.bfloat16

D_MODEL = 1024
D_FF = 2816
N_HEADS = 4
D_HEAD = 128
D_MLSTM = N_HEADS * D_HEAD
D_LRU = 512
N_META = 16
CONV_WIDTH = 4
LRU_C = 8.0
EPS = 1e-6

COL_Q, COL_K, COL_V, COL_O, COL_XR, COL_YG, COL_G = (i * 512 for i in range(7))
GATE_LANES = 128
D_INP = COL_G + GATE_LANES

SUBLANES = 8
MAIN_TILE = 1024
FF_CHUNK = 256
PROMPT_CHUNK = 256
SAMPLE_SEQ_BLOCK = 8
VMEM_LIMIT_BYTES = 56 * 1024 * 1024

NT_DIMS = (((1,), (1,)), ((), ()))
TN_DIMS = (((0,), (0,)), ((), ()))


def _dot(a, b):
    return jnp.dot(a, b, preferred_element_type=F32)


def _sigmoid(x):
    return 0.5 * jnp.tanh(0.5 * x) + 0.5


def _rmsnorm(x, g):
    return x * lax.rsqrt(jnp.mean(x * x, axis=-1, keepdims=True) + EPS) * g


def _resident(shape):
    return pl.BlockSpec(shape, lambda *_: (0,) * len(shape), pipeline_mode=pl.Buffered(1))


def _layer(arr, l):
    return pl.BlockSpec((None,) + arr.shape[1:], lambda *_: (l, 0, 0), pipeline_mode=pl.Buffered(1))


def _params(dimension_semantics):
    return pltpu.CompilerParams(dimension_semantics=dimension_semantics, vmem_limit_bytes=VMEM_LIMIT_BYTES)


def _token_call(body, main_in, tail_in, consts, const_specs, out_widths, scratch_shapes, name):
    n_main = main_in[0].shape[0] // MAIN_TILE
    n_tail = tail_in[0].shape[0]
    n_row, n_const, n_out = len(main_in), len(consts), len(out_widths)

    def kernel(*refs):
        main_refs, tail_refs = refs[:n_row], refs[n_row:2 * n_row]
        const_refs = refs[2 * n_row:2 * n_row + n_const]
        outs = refs[2 * n_row + n_const:]
        main_outs, tail_outs, scratch = outs[:n_out], outs[n_out:2 * n_out], outs[2 * n_out:]
        step = pl.program_id(0)

        @pl.when(step == 0)
        def _():
            body(tail_refs, const_refs, tail_outs, scratch, n_tail)

        @pl.when(step > 0)
        def _():
            body(main_refs, const_refs, main_outs, scratch, MAIN_TILE)

    main_spec = lambda a_w: pl.BlockSpec((MAIN_TILE, a_w), lambda i: (jnp.maximum(i - 1, 0), 0))
    tail_spec = lambda a_w: pl.BlockSpec((n_tail, a_w), lambda i: (0, 0))
    return pl.pallas_call(
        kernel,
        grid=(n_main + 1,),
        in_specs=[main_spec(a.shape[1]) for a in main_in]
        + [pl.BlockSpec((n_tail, a.shape[1]), lambda i: (0, 0), pipeline_mode=pl.Buffered(1)) for a in tail_in]
        + list(const_specs),
        out_specs=[main_spec(w) for w in out_widths] + [tail_spec(w) for w in out_widths],
        out_shape=[jax.ShapeDtypeStruct((n_main * MAIN_TILE, w), F32) for w in out_widths]
        + [jax.ShapeDtypeStruct((n_tail, w), F32) for w in out_widths],
        scratch_shapes=scratch_shapes,
        compiler_params=_params(("arbitrary",)),
        name=name,
    )(*main_in, *tail_in, *consts)


def _ffn_body(row_refs, const_refs, out_refs, scratch, rows, *, mix, final):
    it = iter(const_refs)
    if mix:
        x_ref, hm_ref, yl_ref = row_refs
        wo_ref = next(it)
    else:
        (x_ref,) = row_refs
    g_ref, wg_ref, wu_ref, wd_ref = next(it), next(it), next(it), next(it)
    (o_ref,) = out_refs
    xn_ref = scratch[0].at[0:rows]

    x = x_ref[...]
    if mix:
        x = x + _dot(hm_ref[...], wo_ref[0:D_MLSTM, :]) + _dot(yl_ref[...], wo_ref[D_MLSTM:, :])
    xn_ref[...] = _rmsnorm(x, g_ref[...]).astype(BF16)
    o_ref[...] = x
    for j in range(D_FF // FF_CHUNK):
        cols = slice(j * FF_CHUNK, (j + 1) * FF_CHUNK)
        gate = _dot(xn_ref[...], wg_ref[:, cols])
        up = _dot(xn_ref[...], wu_ref[:, cols])
        act = (gate * _sigmoid(gate) * up).astype(BF16)
        o_ref[...] += 0.5 * _dot(act, wd_ref[cols, :])
    if final:
        o_ref[...] = _rmsnorm(o_ref[...], next(it)[...])


def _ffn(x, g, wg, wu, wd, l, *, mix=None, g_final=None):
    x_main, x_tail = x
    main_in, tail_in, consts, specs = [x_main], [x_tail], [], []
    if mix is not None:
        (hm_main, hm_tail), (yl_main, yl_tail), wo = mix
        main_in += [hm_main, yl_main]
        tail_in += [hm_tail, yl_tail]
        consts.append(wo)
        specs.append(_layer(wo, l))
    consts += [g, wg, wu, wd]
    specs += [_layer(g, l), _layer(wg, l), _layer(wu, l), _layer(wd, l)]
    if g_final is not None:
        consts.append(g_final)
        specs.append(_resident(g_final.shape))
    body = functools.partial(_ffn_body, mix=mix is not None, final=g_final is not None)
    return _token_call(body, main_in, tail_in, consts, specs, [D_MODEL],
                       [pltpu.VMEM((MAIN_TILE, D_MODEL), BF16)], "ffn")


def _inproj_body(row_refs, const_refs, out_refs, scratch, rows):
    (x_ref,) = row_refs
    g_ref, wa_ref, wb_ref, wg_ref = const_refs
    (z_ref,) = out_refs
    xn_ref = scratch[0].at[0:rows]
    xn_ref[...] = _rmsnorm(x_ref[...], g_ref[...]).astype(BF16)
    for c0 in range(0, COL_XR, 512):
        z_ref[:, c0:c0 + 512] = _dot(xn_ref[...], wa_ref[:, c0:c0 + 512])
    for c0 in range(0, COL_G - COL_XR, 512):
        z_ref[:, COL_XR + c0:COL_XR + c0 + 512] = _dot(xn_ref[...], wb_ref[:, c0:c0 + 512])
    z_ref[:, COL_G:] = _dot(xn_ref[...], wg_ref[...])


def _inproj(x, g, w_a, w_b, w_g, l):
    x_main, x_tail = x
    consts = [g, w_a, w_b, w_g]
    return _token_call(_inproj_body, [x_main], [x_tail], consts, [_layer(c, l) for c in consts], [D_INP],
                       [pltpu.VMEM((MAIN_TILE, D_MODEL), BF16)], "inproj")


def _scan_rows(x, op, identity):
    n = x.shape[0]
    row = lax.broadcasted_iota(jnp.int32, x.shape, 0)
    d = 1
    while d < n:
        x = op(x, jnp.where(row >= d, pltpu.roll(x, d, 0), identity))
        d *= 2
    return x


def _gate_lanes(g):
    return g, jax.nn.log_sigmoid(pltpu.roll(g, GATE_LANES - N_HEADS, 1))


def _head_out(h, gout, o):
    h = h * lax.rsqrt(jnp.mean(h * h, axis=-1, keepdims=True) + EPS) * gout
    return _sigmoid(o) * h


def _mlstm_chunk_kernel(q_ref, k_ref, v_ref, o_ref, g_ref, bg_ref, gout_ref, c0_ref, n0_ref, m0_ref,
                        hm_ref, c_out_ref, n_out_ref, m_out_ref, c_s, n_s, m_s, *, chunk, n_chunks):
    step = pl.program_id(1)

    @pl.when(step == 0)
    def _():
        c_s[...] = c0_ref[0]
        n_s[0:N_HEADS, :] = n0_ref[0]
        m_s[0:1, :] = m0_ref[0]

    ig, logf = _gate_lanes(g_ref[...] + bg_ref[...])
    b = _scan_rows(logf, jnp.add, 0.0)
    c = ig - b
    m0 = m_s[0:1, :]
    big_m = jnp.maximum(_scan_rows(c, jnp.maximum, -jnp.inf), m0)
    m = b + big_m
    w_inter = jnp.exp(m0 - big_m)
    exp_neg_m = jnp.exp(-m)
    m_last = big_m[chunk - 1:chunk, :]
    w_state = jnp.exp(c - m_last)
    decay = jnp.exp(m0 - m_last)

    row = lax.broadcasted_iota(jnp.int32, (chunk, chunk), 0)
    col = lax.broadcasted_iota(jnp.int32, (chunk, chunk), 1)
    for h in range(N_HEADS):
        lanes = slice(h * D_HEAD, (h + 1) * D_HEAD)
        one = slice(h, h + 1)
        qf = q_ref[:, lanes]
        kf = k_ref[:, lanes] * (D_HEAD ** -0.5)
        vf = v_ref[:, lanes]
        qb, kb = qf.astype(BF16), kf.astype(BF16)
        c_row = jnp.sum(jnp.where(row == col, c[:, one], 0.0), axis=0, keepdims=True)
        w_intra = jnp.exp(jnp.where(col <= row, c_row - big_m[:, one], -jnp.inf))
        s = lax.dot_general(qb, kb, NT_DIMS, preferred_element_type=F32) * w_intra
        c_prev = c_s[h]
        wi = w_inter[:, one]
        num = _dot(s.astype(BF16), vf.astype(BF16)) + wi * lax.dot_general(
            qb, c_prev.astype(BF16), NT_DIMS, preferred_element_type=F32)
        den = jnp.sum(s, axis=1, keepdims=True) + wi * jnp.sum(qf * n_s[one, :], axis=1, keepdims=True)
        hh = num / jnp.maximum(jnp.abs(den), exp_neg_m[:, one])
        hm_ref[:, lanes] = _head_out(hh, gout_ref[:, lanes], o_ref[:, lanes]).astype(BF16)
        ws = w_state[:, one]
        c_s[h] = decay[:, one] * c_prev + lax.dot_general(
            (ws * vf).astype(BF16), kb, TN_DIMS, preferred_element_type=F32)
        n_s[one, :] = decay[:, one] * n_s[one, :] + jnp.sum(ws * kf, axis=0, keepdims=True)
    m_s[0:1, :] = m[chunk - 1:chunk, :]

    @pl.when(step == n_chunks - 1)
    def _():
        c_out_ref[0] = c_s[...]
        n_out_ref[0] = n_s[0:N_HEADS, :]
        m_out_ref[0] = m_s[0:1, :]


def _group_scan_rows(x, op, identity):
    n, width = x.shape
    groups = n // SUBLANES
    x3 = x.reshape(groups, SUBLANES, width)
    sub = lax.broadcasted_iota(jnp.int32, x3.shape, 1)
    d = 1
    while d < SUBLANES:
        x3 = op(x3, jnp.where(sub >= d, pltpu.roll(x3, d, 1), identity))
        d *= 2
    d = 1
    while d < groups:
        totals = jnp.broadcast_to(x3[:, SUBLANES - 1:SUBLANES, :], x3.shape)
        x3 = op(x3, jnp.concatenate(
            [jnp.full((d, SUBLANES, width), identity, x.dtype), totals[:groups - d]], axis=0))
        d *= 2
    return x3.reshape(n, width)


AUG_ROWS = D_HEAD + 16


def _mlstm_wide_kernel(q_ref, k_ref, v_ref, o_ref, g_ref, bg_ref, gout_ref, c0_ref, n0_ref, m0_ref,
                       hm_ref, c_out_ref, n_out_ref, m_out_ref, c_s, m_s, mrow_s, *, chunk, n_chunks):
    step = pl.program_id(1)
    reps = chunk // GATE_LANES
    sub = lax.broadcasted_iota(jnp.int32, (SUBLANES, GATE_LANES), 0)
    lane = lax.broadcasted_iota(jnp.int32, (SUBLANES, GATE_LANES), 1)
    wide = lambda x: jnp.concatenate([x] * reps, axis=1)
    last = lambda x: jnp.broadcast_to(x[:, chunk - 1:chunk], (SUBLANES, GATE_LANES))

    @pl.when(step == 0)
    def _():
        for h in range(N_HEADS):
            c_s[h, 0:D_HEAD, :] = c0_ref[0, h]
            c_s[h, D_HEAD:, :] = jnp.zeros((AUG_ROWS - D_HEAD, D_HEAD), F32)
            c_s[h, D_HEAD:D_HEAD + 1, :] = n0_ref[0, h:h + 1, :]
        m0_rows = jnp.where(lane == sub, jnp.broadcast_to(m0_ref[0], (SUBLANES, GATE_LANES)), 0.0)
        m_s[...] = jnp.broadcast_to(jnp.sum(m0_rows, axis=1, keepdims=True), (SUBLANES, GATE_LANES))
        mrow_s[0:1, :] = m0_ref[0]

    ig, logf = _gate_lanes(g_ref[...] + bg_ref[...])
    b_col = _group_scan_rows(logf, jnp.add, 0.0)
    c_col = ig - b_col
    big_m_col = jnp.maximum(_group_scan_rows(c_col, jnp.maximum, -jnp.inf), mrow_s[0:1, :])
    m_col = b_col + big_m_col
    heads = lambda x: x.T[0:SUBLANES, :]
    c, big_m, m = heads(c_col), heads(big_m_col), heads(m_col)
    m0 = wide(m_s[...])
    w_inter = jnp.exp(m0 - big_m)
    exp_neg_m = jnp.exp(-m)
    m_last = last(big_m)
    w_state = jnp.exp(c - wide(m_last))
    decay = jnp.exp(m_s[...] - m_last)

    src = lax.broadcasted_iota(jnp.int32, (chunk, chunk), 0)
    tgt = lax.broadcasted_iota(jnp.int32, (chunk, chunk), 1)
    pad_row = lax.broadcasted_iota(jnp.int32, (AUG_ROWS - D_HEAD, chunk), 0)
    for h in range(N_HEADS):
        lanes = slice(h * D_HEAD, (h + 1) * D_HEAD)
        one = slice(h, h + 1)
        qb = q_ref[:, lanes].astype(BF16)
        kb = (k_ref[:, lanes] * (D_HEAD ** -0.5)).astype(BF16)
        v_t = v_ref[:, lanes].T
        w_intra = jnp.exp(jnp.where(src <= tgt, c_col[:, one] - big_m[one, :], -jnp.inf))
        s_t = lax.dot_general(kb, qb, NT_DIMS, preferred_element_type=F32) * w_intra
        c_aug = c_s[h]
        inter = lax.dot_general(c_aug.astype(BF16), qb, NT_DIMS, preferred_element_type=F32)
        wi = w_inter[one, :]
        num = _dot(v_t.astype(BF16), s_t.astype(BF16)) + wi * inter[0:D_HEAD, :]
        den = jnp.sum(s_t, axis=0, keepdims=True) + wi * inter[D_HEAD:D_HEAD + 1, :]
        hh = num * (1.0 / jnp.maximum(jnp.abs(den), exp_neg_m[one, :]))
        hh = hh * lax.rsqrt(jnp.mean(hh * hh, axis=0, keepdims=True) + EPS)
        hm_ref[:, lanes] = (_sigmoid(o_ref[:, lanes]) * (hh.T * gout_ref[:, lanes])).astype(BF16)
        ws = w_state[one, :]
        v_aug = jnp.concatenate(
            [v_t * ws, jnp.where(pad_row == 0, jnp.broadcast_to(ws, pad_row.shape), 0.0)], axis=0)
        c_s[h] = decay[one, :] * c_aug + _dot(v_aug.astype(BF16), kb)
    m_s[...] = last(m)
    mrow_s[0:1, :] = m_col[chunk - 1:chunk, :]

    @pl.when(step == n_chunks - 1)
    def _():
        for h in range(N_HEADS):
            c_out_ref[0, h] = c_s[h, 0:D_HEAD, :]
            n_out_ref[0, h:h + 1, :] = c_s[h, D_HEAD:D_HEAD + 1, :]
        m_out_ref[0] = mrow_s[0:1, :]


def _mlstm_chunked(z, bg, gout, l, state, *, n_seq, seq_len, chunk, row0, name):
    n_chunks = seq_len // chunk
    blk0 = row0 // chunk
    zspec = lambda width, colblk: pl.BlockSpec((chunk, width), lambda b, c: (blk0 + b * n_chunks + c, colblk))
    c0, n0, m0 = state
    if chunk % GATE_LANES == 0:
        body = _mlstm_wide_kernel
        scratch = [pltpu.VMEM((N_HEADS, AUG_ROWS, D_HEAD), F32), pltpu.VMEM((SUBLANES, GATE_LANES), F32),
                   pltpu.VMEM((SUBLANES, GATE_LANES), F32)]
    else:
        body = _mlstm_chunk_kernel
        scratch = [pltpu.VMEM((N_HEADS, D_HEAD, D_HEAD), F32), pltpu.VMEM((SUBLANES, D_HEAD), F32),
                   pltpu.VMEM((SUBLANES, GATE_LANES), F32)]
    return pl.pallas_call(
        functools.partial(body, chunk=chunk, n_chunks=n_chunks),
        grid=(n_seq, n_chunks),
        in_specs=[zspec(512, COL_Q // 512), zspec(512, COL_K // 512), zspec(512, COL_V // 512),
                  zspec(512, COL_O // 512), zspec(GATE_LANES, COL_G // GATE_LANES),
                  _layer(bg, l), _layer(gout, l),
                  pl.BlockSpec((1, N_HEADS, D_HEAD, D_HEAD), lambda b, c: (0, 0, 0, 0)),
                  pl.BlockSpec((1, N_HEADS, D_HEAD), lambda b, c: (0, 0, 0)),
                  pl.BlockSpec((1, 1, GATE_LANES), lambda b, c: (0, 0, 0))],
        out_specs=[pl.BlockSpec((chunk, D_MLSTM), lambda b, c: (b * n_chunks + c, 0)),
                   pl.BlockSpec((1, N_HEADS, D_HEAD, D_HEAD), lambda b, c: (b, 0, 0, 0)),
                   pl.BlockSpec((1, N_HEADS, D_HEAD), lambda b, c: (b, 0, 0)),
                   pl.BlockSpec((1, 1, GATE_LANES), lambda b, c: (b, 0, 0))],
        out_shape=[jax.ShapeDtypeStruct((n_seq * seq_len, D_MLSTM), BF16),
                   jax.ShapeDtypeStruct((n_seq, N_HEADS, D_HEAD, D_HEAD), F32),
                   jax.ShapeDtypeStruct((n_seq, N_HEADS, D_HEAD), F32),
                   jax.ShapeDtypeStruct((n_seq, 1, GATE_LANES), F32)],
        scratch_shapes=scratch,
        compiler_params=_params(("arbitrary", "arbitrary")),
        name=name,
    )(z, z, z, z, z, bg, gout, c0, n0, m0)


def _mlstm_step_kernel(*refs, n_seq, n_tok, aliased):
    nh = N_HEADS
    q_refs, k_refs, v_refs, o_refs = refs[0:nh], refs[nh:2 * nh], refs[2 * nh:3 * nh], refs[3 * nh:4 * nh]
    g_ref, bg_ref, gout_ref, c0_ref, n0_ref, m0_ref = refs[4 * nh:4 * nh + 6]
    hm_ref, c_out_ref, n_out_ref, m_out_ref, inter_s, wv_s, hm_s = refs[4 * nh + 6 + int(aliased):]

    def token(ref, t):
        return ref[pl.ds(t, n_seq, stride=n_tok), :]

    m0 = m0_ref[...]
    b, c, big_m = [], [], []
    for t in range(n_tok):
        ig, logf = _gate_lanes(token(g_ref, t) + bg_ref[...])
        b.append(logf if t == 0 else b[-1] + logf)
        c.append(ig - b[-1])
        big_m.append(jnp.maximum(c[-1], m0 if t == 0 else big_m[-1]))
    m_last = big_m[-1]
    decay = jnp.exp(m0 - m_last)
    scale = D_HEAD ** -0.5

    for j in range(n_seq):
        rows = slice(j * n_tok, (j + 1) * n_tok)
        for h in range(nh):
            inter_s[h, rows, :] = lax.dot_general(
                q_refs[h][rows, :].astype(BF16), c0_ref[j, h].astype(BF16), NT_DIMS, preferred_element_type=F32)

    for h in range(nh):
        lanes = slice(h * D_HEAD, (h + 1) * D_HEAD)
        one = slice(h, h + 1)
        q = [token(q_refs[h], t) for t in range(n_tok)]
        k = [token(k_refs[h], t) * scale for t in range(n_tok)]
        v = [token(v_refs[h], t) for t in range(n_tok)]
        n0 = n0_ref[:, lanes]
        for t in range(n_tok):
            mt = big_m[t][:, one]
            wi = jnp.exp(m0[:, one] - mt)
            num = wi * inter_s[h, pl.ds(t, n_seq, stride=n_tok), :]
            den = wi * jnp.sum(q[t] * n0, axis=1, keepdims=True)
            for s in range(t + 1):
                w = jnp.sum(q[t] * k[s], axis=1, keepdims=True) * jnp.exp(c[s][:, one] - mt)
                num = num + w * v[s]
                den = den + w
            hh = num / jnp.maximum(jnp.abs(den), jnp.exp(-(b[t][:, one] + mt)))
            hm_s[h, pl.ds(t, n_seq, stride=n_tok), :] = _head_out(hh, gout_ref[:, lanes], token(o_refs[h], t))
        n_new = decay[:, one] * n0
        for s in range(n_tok):
            ws = jnp.exp(c[s][:, one] - m_last[:, one])
            wv_s[h, pl.ds(s, n_seq, stride=n_tok), :] = ws * v[s]
            n_new = n_new + ws * k[s]
        n_out_ref[:, lanes] = n_new
        hm_ref[:, lanes] = hm_s[h].astype(BF16)
    m_out_ref[...] = b[-1] + m_last

    for j in range(n_seq):
        rows = slice(j * n_tok, (j + 1) * n_tok)
        for h in range(nh):
            kb = (k_refs[h][rows, :] * scale).astype(BF16)
            c_out_ref[j, h] = decay[j:j + 1, h:h + 1] * c0_ref[j, h] + lax.dot_general(
                wv_s[h, rows, :].astype(BF16), kb, TN_DIMS, preferred_element_type=F32)


def _mlstm_step(z, bg, gout, l, c_all, n0, m0, c_out_prev, *, n_seq, n_tok):
    sb = SAMPLE_SEQ_BLOCK
    rows = sb * n_tok
    head_specs = lambda col0: [pl.BlockSpec((rows, D_HEAD), functools.partial(
        lambda i, cb: (i, cb), cb=col0 // D_HEAD + h)) for h in range(N_HEADS)]
    c_spec = pl.BlockSpec((None, sb, N_HEADS, D_HEAD, D_HEAD), lambda i: (l, i, 0, 0, 0))
    n_z = 4 * N_HEADS + 1
    args = [z] * n_z + [bg, gout, c_all, n0, m0]
    specs = head_specs(COL_Q) + head_specs(COL_K) + head_specs(COL_V) + head_specs(COL_O) + [
        pl.BlockSpec((rows, GATE_LANES), lambda i: (i, COL_G // GATE_LANES)),
        _layer(bg, l), _layer(gout, l), c_spec,
        pl.BlockSpec((sb, D_MLSTM), lambda i: (i, 0)),
        pl.BlockSpec((sb, GATE_LANES), lambda i: (i, 0))]
    aliases = {}
    if c_out_prev is not None:
        args.append(c_out_prev)
        specs.append(pl.BlockSpec(memory_space=pl.ANY))
        aliases = {len(args) - 1: 1}
    return pl.pallas_call(
        functools.partial(_mlstm_step_kernel, n_seq=sb, n_tok=n_tok, aliased=c_out_prev is not None),
        grid=(n_seq // sb,),
        in_specs=specs,
        out_specs=[pl.BlockSpec((rows, D_MLSTM), lambda i: (i, 0)), c_spec,
                   pl.BlockSpec((sb, D_MLSTM), lambda i: (i, 0)),
                   pl.BlockSpec((sb, GATE_LANES), lambda i: (i, 0))],
        out_shape=[jax.ShapeDtypeStruct((n_seq * n_tok, D_MLSTM), BF16),
                   jax.ShapeDtypeStruct(c_all.shape, F32),
                   jax.ShapeDtypeStruct(n0.shape, F32),
                   jax.ShapeDtypeStruct(m0.shape, F32)],
        scratch_shapes=[pltpu.VMEM((N_HEADS, rows, D_HEAD), F32)] * 3,
        input_output_aliases=aliases,
        compiler_params=_params(("arbitrary",)),
        name="mlstm_step",
    )(*args)


def _expm1(y):
    return jnp.tanh(0.5 * y) * (jnp.exp(y) + 1.0)


def _lru_gates(xc, wa_ref, ba_ref, wx_ref, bx_ref, lam_ref):
    xb = xc.astype(BF16)
    r = _sigmoid(_dot(xb, wa_ref[...]) + ba_ref[...])
    i = _sigmoid(_dot(xb, wx_ref[...]) + bx_ref[...])
    log_a = -LRU_C * r * jax.nn.softplus(-lam_ref[...])
    return jnp.exp(log_a), jnp.sqrt(-_expm1(2.0 * log_a)) * (i * xc)


def _lru_out(h, g, yg):
    h = h * lax.rsqrt(jnp.mean(h * h, axis=-1, keepdims=True) + EPS) * g
    return h * jax.nn.gelu(yg)


def _lru_chunk_kernel(xr_ref, yg_ref, cw_ref, cb_ref, wa_ref, ba_ref, wx_ref, bx_ref, lam_ref, gl_ref,
                      h0_ref, buf0_ref, yl_ref, h_out_ref, buf_out_ref, prev_s, carry_s, h_s, *, chunk, n_chunks):
    step = pl.program_id(1)
    taps = CONV_WIDTH - 1

    @pl.when(step == 0)
    def _():
        prev_s[...] = jnp.zeros(prev_s.shape, F32)
        prev_s[SUBLANES - taps:SUBLANES, :] = buf0_ref[0]
        carry_s[0:1, :] = h0_ref[0]

    ext = jnp.concatenate([prev_s[...], xr_ref[...]], axis=0)
    acc = ext * cw_ref[0:1, :]
    for j in range(1, CONV_WIDTH):
        acc = pltpu.roll(acc, 1, 0) + ext * cw_ref[j:j + 1, :]
    xc = acc[SUBLANES:, :] + cb_ref[...]
    prev_s[...] = ext[chunk:, :]
    a, bx = _lru_gates(xc, wa_ref, ba_ref, wx_ref, bx_ref, lam_ref)

    groups = chunk // SUBLANES
    a3 = a.reshape(groups, SUBLANES, D_LRU)
    b3 = bx.reshape(groups, SUBLANES, D_LRU)
    sub = lax.broadcasted_iota(jnp.int32, a3.shape, 1)
    d = 1
    while d < SUBLANES:
        live = sub >= d
        b3 = a3 * jnp.where(live, pltpu.roll(b3, d, 1), 0.0) + b3
        a3 = a3 * jnp.where(live, pltpu.roll(a3, d, 1), 1.0)
        d *= 2
    carry = carry_s[0:1, :]
    for g in range(groups):
        hg = b3[g] + a3[g] * carry
        h_s[g * SUBLANES:(g + 1) * SUBLANES, :] = hg
        carry = hg[SUBLANES - 1:SUBLANES, :]
    carry_s[0:1, :] = carry

    yl_ref[...] = _lru_out(h_s[...], gl_ref[...], yg_ref[...]).astype(BF16)

    @pl.when(step == n_chunks - 1)
    def _():
        h_out_ref[0] = carry_s[0:1, :]
        buf_out_ref[0] = prev_s[SUBLANES - taps:SUBLANES, :]


def _lru_chunked(z, lw, l, state, *, n_seq, seq_len, chunk, row0, name):
    n_chunks = seq_len // chunk
    blk0 = row0 // chunk
    zspec = lambda colblk: pl.BlockSpec((chunk, D_LRU), lambda b, c: (blk0 + b * n_chunks + c, colblk))
    h0, buf0 = state
    return pl.pallas_call(
        functools.partial(_lru_chunk_kernel, chunk=chunk, n_chunks=n_chunks),
        grid=(n_seq, n_chunks),
        in_specs=[zspec(COL_XR // D_LRU), zspec(COL_YG // D_LRU)] + [_layer(w, l) for w in lw] + [
            pl.BlockSpec((1, 1, D_LRU), lambda b, c: (0, 0, 0)),
            pl.BlockSpec((1, CONV_WIDTH - 1, D_LRU), lambda b, c: (0, 0, 0))],
        out_specs=[pl.BlockSpec((chunk, D_LRU), lambda b, c: (b * n_chunks + c, 0)),
                   pl.BlockSpec((1, 1, D_LRU), lambda b, c: (b, 0, 0)),
                   pl.BlockSpec((1, CONV_WIDTH - 1, D_LRU), lambda b, c: (b, 0, 0))],
        out_shape=[jax.ShapeDtypeStruct((n_seq * seq_len, D_LRU), BF16),
                   jax.ShapeDtypeStruct((n_seq, 1, D_LRU), F32),
                   jax.ShapeDtypeStruct((n_seq, CONV_WIDTH - 1, D_LRU), F32)],
        scratch_shapes=[pltpu.VMEM((SUBLANES, D_LRU), F32), pltpu.VMEM((SUBLANES, D_LRU), F32),
                        pltpu.VMEM((chunk, D_LRU), F32)],
        compiler_params=_params(("arbitrary", "arbitrary")),
        name=name,
    )(z, z, *lw, h0, buf0)


def _lru_step_kernel(*refs, n_seq, n_tok):
    nb = D_LRU // 128
    xr_refs, yg_refs = refs[0:nb], refs[nb:2 * nb]
    (cw_ref, cb_ref, wa_ref, ba_ref, wx_ref, bx_ref, lam_ref, gl_ref, h0_ref, buf0_ref,
     yl_ref, h_out_ref, buf_out_ref, xc_s, yg_s, yl_s) = refs[2 * nb:]
    taps = CONV_WIDTH - 1
    for cblk in range(nb):
        lanes = slice(cblk * 128, (cblk + 1) * 128)
        xp = [buf0_ref[:, j * D_LRU + cblk * 128:j * D_LRU + (cblk + 1) * 128] for j in range(taps)]
        xp += [xr_refs[cblk][pl.ds(t, n_seq, stride=n_tok), :] for t in range(n_tok)]
        for t in range(n_tok):
            xc = cb_ref[:, lanes]
            for j in range(CONV_WIDTH):
                xc = xc + xp[t + j] * cw_ref[j:j + 1, lanes]
            xc_s[t * n_seq:(t + 1) * n_seq, lanes] = xc
            yg_s[t * n_seq:(t + 1) * n_seq, lanes] = yg_refs[cblk][pl.ds(t, n_seq, stride=n_tok), :]
        for j in range(taps):
            buf_out_ref[:, j * D_LRU + cblk * 128:j * D_LRU + (cblk + 1) * 128] = xp[n_tok + j]
    a, bx = _lru_gates(xc_s[...], wa_ref, ba_ref, wx_ref, bx_ref, lam_ref)
    h = h0_ref[...]
    for t in range(n_tok):
        rows = slice(t * n_seq, (t + 1) * n_seq)
        h = a[rows] * h + bx[rows]
        y = _lru_out(h, gl_ref[...], yg_s[rows, :])
        for cblk in range(nb):
            yl_s[cblk, pl.ds(t, n_seq, stride=n_tok), :] = y[:, cblk * 128:(cblk + 1) * 128]
    for cblk in range(nb):
        yl_ref[:, cblk * 128:(cblk + 1) * 128] = yl_s[cblk].astype(BF16)
    h_out_ref[...] = h


def _lru_step(z, lw, l, state, *, n_seq, n_tok):
    rows = n_seq * n_tok
    nb = D_LRU // 128
    h0, buf0 = state
    whole = lambda a: pl.BlockSpec(a.shape, lambda i: (0,) * a.ndim)
    lane_specs = lambda col0: [pl.BlockSpec((rows, 128), functools.partial(
        lambda i, cb: (0, cb), cb=col0 // 128 + c)) for c in range(nb)]
    return pl.pallas_call(
        functools.partial(_lru_step_kernel, n_seq=n_seq, n_tok=n_tok),
        grid=(1,),
        in_specs=lane_specs(COL_XR) + lane_specs(COL_YG) + [_layer(w, l) for w in lw] + [whole(h0), whole(buf0)],
        out_specs=[pl.BlockSpec((rows, D_LRU), lambda i: (0, 0)), whole(h0), whole(buf0)],
        out_shape=[jax.ShapeDtypeStruct((rows, D_LRU), BF16),
                   jax.ShapeDtypeStruct(h0.shape, F32),
                   jax.ShapeDtypeStruct(buf0.shape, F32)],
        scratch_shapes=[pltpu.VMEM((rows, D_LRU), F32), pltpu.VMEM((rows, D_LRU), F32),
                        pltpu.VMEM((nb, rows, 128), F32)],
        compiler_params=_params(("arbitrary",)),
        name="lru_step",
    )(*([z] * (2 * nb)), *lw, h0, buf0)


def _block_diag(w):
    depth, n, d, _ = w.shape
    eye = jnp.eye(n, dtype=w.dtype)
    return (eye[None, :, None, :, None] * w[:, :, :, None, :]).reshape(depth, n * d, n * d)


def kernel(x_prompt, x_sample, state_mlstm_C, state_mlstm_n, state_mlstm_m, state_lru_h, state_conv, meta_tokens, g_ff1, w_ff1_gate, w_ff1_up, w_ff1_down, g_mix, w_in, b_gates, conv_w, conv_b, w_rg_a, b_rg_a, w_rg_x, b_rg_x, lru_lambda, g_mlstm_out, g_lru_out, w_out, g_ff2, w_ff2_gate, w_ff2_up, w_ff2_down, g_final):
    batch, seq, _ = x_prompt.shape
    n_samp, n_tok, _ = x_sample.shape
    depth = w_in.shape[0]
    n_prompt = batch * seq
    n_dec = n_samp * n_tok
    assert n_prompt % MAIN_TILE == 0 and seq % PROMPT_CHUNK == 0
    assert n_dec % N_META == 0 and n_samp % SAMPLE_SEQ_BLOCK == 0
    taps = CONV_WIDTH - 1

    x = (x_prompt.reshape(n_prompt, D_MODEL),
         jnp.concatenate([x_sample.reshape(n_dec, D_MODEL), meta_tokens.astype(F32)], axis=0))

    vec = lambda a: a.astype(F32).reshape(depth, 1, -1)
    ff1 = (vec(g_ff1), w_ff1_gate.astype(BF16), w_ff1_up.astype(BF16), w_ff1_down.astype(BF16))
    ff2 = (vec(g_ff2), w_ff2_gate.astype(BF16), w_ff2_up.astype(BF16), w_ff2_down.astype(BF16))
    n_gate = 2 * N_HEADS
    w_a = w_in[:, :, :4 * D_MLSTM].astype(BF16)
    w_b = w_in[:, :, 4 * D_MLSTM + n_gate:].astype(BF16)
    w_g = jnp.pad(w_in[:, :, 4 * D_MLSTM:4 * D_MLSTM + n_gate], ((0, 0), (0, 0), (0, GATE_LANES - n_gate))).astype(BF16)
    bg = jnp.pad(b_gates.astype(F32), ((0, 0), (0, GATE_LANES - n_gate))).reshape(depth, 1, GATE_LANES)
    gout = vec(g_mlstm_out)
    wo = w_out.astype(BF16)
    lw = (conv_w.astype(F32), vec(conv_b), _block_diag(w_rg_a).astype(BF16), vec(b_rg_a),
          _block_diag(w_rg_x).astype(BF16), vec(b_rg_x), vec(lru_lambda), vec(g_lru_out))
    g_fin = g_final.astype(F32).reshape(1, D_MODEL)

    zero_state = (jnp.zeros((1, N_HEADS, D_HEAD, D_HEAD), F32), jnp.zeros((1, N_HEADS, D_HEAD), F32),
                  jnp.zeros((1, 1, GATE_LANES), F32))
    zero_lru = (jnp.zeros((1, 1, D_LRU), F32), jnp.zeros((1, taps, D_LRU), F32))
    c_all = state_mlstm_C.astype(F32)
    n_all = state_mlstm_n.astype(F32).reshape(depth, n_samp, D_MLSTM)
    m_all = jnp.pad(state_mlstm_m.astype(F32), ((0, 0), (0, 0), (0, GATE_LANES - N_HEADS)))
    h_all = state_lru_h.astype(F32)
    buf_all = state_conv.astype(F32).reshape(depth, n_samp, taps * D_LRU)

    outs = {k: [] for k in ("pC", "pn", "pm", "ph", "pb", "sn", "sm", "sh", "sb")}
    s_c = None
    for l in range(depth):
        x = _ffn(x, *ff1, l)
        z_main, z_tail = _inproj(x, vec(g_mix), w_a, w_b, w_g, l)

        hm_meta, m_c, m_n, m_m = _mlstm_chunked(z_tail, bg, gout, l, zero_state, n_seq=1, seq_len=N_META,
                                                chunk=N_META, row0=n_dec, name="mlstm_meta")
        hm_main, p_c, p_n, p_m = _mlstm_chunked(z_main, bg, gout, l, (m_c, m_n, m_m), n_seq=batch, seq_len=seq,
                                                chunk=PROMPT_CHUNK, row0=0, name="mlstm_prompt")
        hm_dec, s_c, s_n, s_m = _mlstm_step(z_tail, bg, gout, l, c_all, n_all[l], m_all[l], s_c,
                                            n_seq=n_samp, n_tok=n_tok)

        yl_meta, m_h, m_b = _lru_chunked(z_tail, lw, l, zero_lru, n_seq=1, seq_len=N_META, chunk=N_META,
                                         row0=n_dec, name="lru_meta")
        yl_main, p_h, p_b = _lru_chunked(z_main, lw, l, (m_h, m_b), n_seq=batch, seq_len=seq,
                                         chunk=PROMPT_CHUNK, row0=0, name="lru_prompt")
        yl_dec, s_h, s_b = _lru_step(z_tail, lw, l, (h_all[l], buf_all[l]), n_seq=n_samp, n_tok=n_tok)

        mix = ((hm_main, jnp.concatenate([hm_dec, hm_meta], axis=0)),
               (yl_main, jnp.concatenate([yl_dec, yl_meta], axis=0)), wo)
        x = _ffn(x, *ff2, l, mix=mix, g_final=g_fin if l == depth - 1 else None)

        for key, val in (("pC", p_c), ("pn", p_n), ("pm", p_m[:, 0, :N_HEADS]), ("ph", p_h[:, 0]), ("pb", p_b),
                         ("sn", s_n.reshape(n_samp, N_HEADS, D_HEAD)), ("sm", s_m[:, :N_HEADS]),
                         ("sh", s_h), ("sb", s_b.reshape(n_samp, taps, D_LRU))):
            outs[key].append(val)

    y_main, y_tail = x
    st = {k: jnp.stack(v) for k, v in outs.items()}
    return (y_main.reshape(batch, seq, D_MODEL), y_tail[:n_dec].reshape(n_samp, n_tok, D_MODEL),
            st["pC"], st["pn"], st["pm"], st["ph"], st["pb"], s_c, st["sn"], st["sm"], st["sh"], st["sb"])
```

```python
import functools

import jax
import jax.numpy as jnp
from jax import lax
from jax.experimental import pallas as pl
from jax.experimental.pallas import tpu as pltpu

F32 = jnp.float32
BF16 = jnp.bfloat16

D_MODEL = 1024
D_FF = 2816
N_HEADS = 4
D_HEAD = 128
D_MLSTM = N_HEADS * D_HEAD
D_LRU = 512
N_META = 16
CONV_WIDTH = 4
LRU_C = 8.0
EPS = 1e-6

COL_Q, COL_K, COL_V, COL_O, COL_XR, COL_YG, COL_G = (i * 512 for i in range(7))
GATE_LANES = 128
D_INP = COL_G + GATE_LANES

SUBLANES = 8
MAIN_TILE = 1024
FF_CHUNK = 256
PROMPT_CHUNK = 256
SAMPLE_SEQ_BLOCK = 8
VMEM_LIMIT_BYTES = 56 * 1024 * 1024

NT_DIMS = (((1,), (1,)), ((), ()))
TN_DIMS = (((0,), (0,)), ((), ()))


def _dot(a, b):
    return jnp.dot(a, b, preferred_element_type=F32)


def _sigmoid(x):
    return 0.5 * jnp.tanh(0.5 * x) + 0.5


def _rmsnorm(x, g):
    return x * lax.rsqrt(jnp.mean(x * x, axis=-1, keepdims=True) + EPS) * g


def _resident(shape):
    return pl.BlockSpec(shape, lambda *_: (0,) * len(shape), pipeline_mode=pl.Buffered(1))


def _layer(arr, l):
    return pl.BlockSpec((None,) + arr.shape[1:], lambda *_: (l, 0, 0), pipeline_mode=pl.Buffered(1))


def _params(dimension_semantics):
    return pltpu.CompilerParams(dimension_semantics=dimension_semantics, vmem_limit_bytes=VMEM_LIMIT_BYTES)


def _token_call(body, main_in, tail_in, consts, const_specs, out_widths, scratch_shapes, name):
    n_main = main_in[0].shape[0] // MAIN_TILE
    n_tail = tail_in[0].shape[0]
    n_row, n_const, n_out = len(main_in), len(consts), len(out_widths)

    def kernel(*refs):
        main_refs, tail_refs = refs[:n_row], refs[n_row:2 * n_row]
        const_refs = refs[2 * n_row:2 * n_row + n_const]
        outs = refs[2 * n_row + n_const:]
        main_outs, tail_outs, scratch = outs[:n_out], outs[n_out:2 * n_out], outs[2 * n_out:]
        step = pl.program_id(0)

        @pl.when(step == 0)
        def _():
            body(tail_refs, const_refs, tail_outs, scratch, n_tail)

        @pl.when(step > 0)
        def _():
            body(main_refs, const_refs, main_outs, scratch, MAIN_TILE)

    main_spec = lambda a_w: pl.BlockSpec((MAIN_TILE, a_w), lambda i: (jnp.maximum(i - 1, 0), 0))
    tail_spec = lambda a_w: pl.BlockSpec((n_tail, a_w), lambda i: (0, 0))
    return pl.pallas_call(
        kernel,
        grid=(n_main + 1,),
        in_specs=[main_spec(a.shape[1]) for a in main_in]
        + [pl.BlockSpec((n_tail, a.shape[1]), lambda i: (0, 0), pipeline_mode=pl.Buffered(1)) for a in tail_in]
        + list(const_specs),
        out_specs=[main_spec(w) for w in out_widths] + [tail_spec(w) for w in out_widths],
        out_shape=[jax.ShapeDtypeStruct((n_main * MAIN_TILE, w), F32) for w in out_widths]
        + [jax.ShapeDtypeStruct((n_tail, w), F32) for w in out_widths],
        scratch_shapes=scratch_shapes,
        compiler_params=_params(("arbitrary",)),
        name=name,
    )(*main_in, *tail_in, *consts)


def _ffn_body(row_refs, const_refs, out_refs, scratch, rows, *, mix, final):
    it = iter(const_refs)
    if mix:
        x_ref, hm_ref, yl_ref = row_refs
        wo_ref = next(it)
    else:
        (x_ref,) = row_refs
    g_ref, wg_ref, wu_ref, wd_ref = next(it), next(it), next(it), next(it)
    (o_ref,) = out_refs
    xn_ref = scratch[0].at[0:rows]

    x = x_ref[...]
    if mix:
        x = x + _dot(hm_ref[...], wo_ref[0:D_MLSTM, :]) + _dot(yl_ref[...], wo_ref[D_MLSTM:, :])
    xn_ref[...] = _rmsnorm(x, g_ref[...]).astype(BF16)
    o_ref[...] = x
    for j in range(D_FF // FF_CHUNK):
        cols = slice(j * FF_CHUNK, (j + 1) * FF_CHUNK)
        gate = _dot(xn_ref[...], wg_ref[:, cols])
        up = _dot(xn_ref[...], wu_ref[:, cols])
        act = (gate * _sigmoid(gate) * up).astype(BF16)
        o_ref[...] += 0.5 * _dot(act, wd_ref[cols, :])
    if final:
        o_ref[...] = _rmsnorm(o_ref[...], next(it)[...])


def _ffn(x, g, wg, wu, wd, l, *, mix=None, g_final=None):
    x_main, x_tail = x
    main_in, tail_in, consts, specs = [x_main], [x_tail], [], []
    if mix is not None:
        (hm_main, hm_tail), (yl_main, yl_tail), wo = mix
        main_in += [hm_main, yl_main]
        tail_in += [hm_tail, yl_tail]
        consts.append(wo)
        specs.append(_layer(wo, l))
    consts += [g, wg, wu, wd]
    specs += [_layer(g, l), _layer(wg, l), _layer(wu, l), _layer(wd, l)]
    if g_final is not None:
        consts.append(g_final)
        specs.append(_resident(g_final.shape))
    body = functools.partial(_ffn_body, mix=mix is not None, final=g_final is not None)
    return _token_call(body, main_in, tail_in, consts, specs, [D_MODEL],
                       [pltpu.VMEM((MAIN_TILE, D_MODEL), BF16)], "ffn")


def _scan_rows(x, op, identity):
    n = x.shape[0]
    row = lax.broadcasted_iota(jnp.int32, x.shape, 0)
    d = 1
    while d < n:
        x = op(x, jnp.where(row >= d, pltpu.roll(x, d, 0), identity))
        d *= 2
    return x


def _gate_lanes(g):
    return g, jax.nn.log_sigmoid(pltpu.roll(g, GATE_LANES - N_HEADS, 1))


def _head_out(h, gout, o):
    h = h * lax.rsqrt(jnp.mean(h * h, axis=-1, keepdims=True) + EPS) * gout
    return _sigmoid(o) * h


def _mlstm_chunk_kernel(q_ref, k_ref, v_ref, o_ref, g_ref, bg_ref, gout_ref, c0_ref, n0_ref, m0_ref,
                        hm_ref, c_out_ref, n_out_ref, m_out_ref, c_s, n_s, m_s, *, chunk, n_chunks):
    step = pl.program_id(1)

    @pl.when(step == 0)
    def _():
        c_s[...] = c0_ref[0]
        n_s[0:N_HEADS, :] = n0_ref[0]
        m_s[0:1, :] = m0_ref[0]

    ig, logf = _gate_lanes(g_ref[...] + bg_ref[...])
    b = _scan_rows(logf, jnp.add, 0.0)
    c = ig - b
    m0 = m_s[0:1, :]
    big_m = jnp.maximum(_scan_rows(c, jnp.maximum, -jnp.inf), m0)
    m = b + big_m
    w_inter = jnp.exp(m0 - big_m)
    exp_neg_m = jnp.exp(-m)
    m_last = big_m[chunk - 1:chunk, :]
    w_state = jnp.exp(c - m_last)
    decay = jnp.exp(m0 - m_last)

    row = lax.broadcasted_iota(jnp.int32, (chunk, chunk), 0)
    col = lax.broadcasted_iota(jnp.int32, (chunk, chunk), 1)
    for h in range(N_HEADS):
        lanes = slice(h * D_HEAD, (h + 1) * D_HEAD)
        one = slice(h, h + 1)
        qf = q_ref[:, lanes]
        kf = k_ref[:, lanes] * (D_HEAD ** -0.5)
        vf = v_ref[:, lanes]
        qb, kb = qf.astype(BF16), kf.astype(BF16)
        c_row = jnp.sum(jnp.where(row == col, c[:, one], 0.0), axis=0, keepdims=True)
        w_intra = jnp.exp(jnp.where(col <= row, c_row - big_m[:, one], -jnp.inf))
        s = lax.dot_general(qb, kb, NT_DIMS, preferred_element_type=F32) * w_intra
        c_prev = c_s[h]
        wi = w_inter[:, one]
        num = _dot(s.astype(BF16), vf.astype(BF16)) + wi * lax.dot_general(
            qb, c_prev.astype(BF16), NT_DIMS, preferred_element_type=F32)
        den = jnp.sum(s, axis=1, keepdims=True) + wi * jnp.sum(qf * n_s[one, :], axis=1, keepdims=True)
        hh = num / jnp.maximum(jnp.abs(den), exp_neg_m[:, one])
        hm_ref[:, lanes] = _head_out(hh, gout_ref[:, lanes], o_ref[:, lanes]).astype(BF16)
        ws = w_state[:, one]
        c_s[h] = decay[:, one] * c_prev + lax.dot_general(
            (ws * vf).astype(BF16), kb, TN_DIMS, preferred_element_type=F32)
        n_s[one, :] = decay[:, one] * n_s[one, :] + jnp.sum(ws * kf, axis=0, keepdims=True)
    m_s[0:1, :] = m[chunk - 1:chunk, :]

    @pl.when(step == n_chunks - 1)
    def _():
        c_out_ref[0] = c_s[...]
        n_out_ref[0] = n_s[0:N_HEADS, :]
        m_out_ref[0] = m_s[0:1, :]


def _group_scan_rows(x, op, identity):
    n, width = x.shape
    groups = n // SUBLANES
    x3 = x.reshape(groups, SUBLANES, width)
    sub = lax.broadcasted_iota(jnp.int32, x3.shape, 1)
    d = 1
    while d < SUBLANES:
        x3 = op(x3, jnp.where(sub >= d, pltpu.roll(x3, d, 1), identity))
        d *= 2
    d = 1
    while d < groups:
        totals = jnp.broadcast_to(x3[:, SUBLANES - 1:SUBLANES, :], x3.shape)
        x3 = op(x3, jnp.concatenate(
            [jnp.full((d, SUBLANES, width), identity, x.dtype), totals[:groups - d]], axis=0))
        d *= 2
    return x3.reshape(n, width)


AUG_ROWS = D_HEAD + 16


def _mlstm_wide_kernel(q_ref, k_ref, v_ref, o_ref, g_ref, bg_ref, gout_ref, c0_ref, n0_ref, m0_ref,
                       hm_ref, c_out_ref, n_out_ref, m_out_ref, c_s, m_s, mrow_s, *, chunk, n_chunks):
    step = pl.program_id(1)
    reps = chunk // GATE_LANES
    sub = lax.broadcasted_iota(jnp.int32, (SUBLANES, GATE_LANES), 0)
    lane = lax.broadcasted_iota(jnp.int32, (SUBLANES, GATE_LANES), 1)
    wide = lambda x: jnp.concatenate([x] * reps, axis=1)
    last = lambda x: jnp.broadcast_to(x[:, chunk - 1:chunk], (SUBLANES, GATE_LANES))

    @pl.when(step == 0)
    def _():
        for h in range(N_HEADS):
            c_s[h, 0:D_HEAD, :] = c0_ref[0, h]
            c_s[h, D_HEAD:, :] = jnp.zeros((AUG_ROWS - D_HEAD, D_HEAD), F32)
            c_s[h, D_HEAD:D_HEAD + 1, :] = n0_ref[0, h:h + 1, :]
        m0_rows = jnp.where(lane == sub, jnp.broadcast_to(m0_ref[0], (SUBLANES, GATE_LANES)), 0.0)
        m_s[...] = jnp.broadcast_to(jnp.sum(m0_rows, axis=1, keepdims=True), (SUBLANES, GATE_LANES))
        mrow_s[0:1, :] = m0_ref[0]

    qb, kb, v_t, s_raw, inter_all = [], [], [], [], []
    for h in range(N_HEADS):
        lanes = slice(h * D_HEAD, (h + 1) * D_HEAD)
        qb.append(q_ref[:, lanes].astype(BF16))
        kb.append((k_ref[:, lanes] * (D_HEAD ** -0.5)).astype(BF16))
        v_t.append(v_ref[:, lanes].T)
        s_raw.append(lax.dot_general(kb[h], qb[h], NT_DIMS, preferred_element_type=F32))
        inter_all.append(lax.dot_general(c_s[h].astype(BF16), qb[h], NT_DIMS, preferred_element_type=F32))

    ig, logf = _gate_lanes(g_ref[...] + bg_ref[...])
    b_col = _group_scan_rows(logf, jnp.add, 0.0)
    c_col = ig - b_col
    big_m_col = jnp.maximum(_group_scan_rows(c_col, jnp.maximum, -jnp.inf), mrow_s[0:1, :])
    m_col = b_col + big_m_col
    heads = lambda x: x.T[0:SUBLANES, :]
    c, big_m, m = heads(c_col), heads(big_m_col), heads(m_col)
    m0 = wide(m_s[...])
    w_inter = jnp.exp(m0 - big_m)
    exp_neg_m = jnp.exp(-m)
    m_last = last(big_m)
    w_state = jnp.exp(c - wide(m_last))
    decay = jnp.exp(m_s[...] - m_last)

    src = lax.broadcasted_iota(jnp.int32, (chunk, chunk), 0)
    tgt = lax.broadcasted_iota(jnp.int32, (chunk, chunk), 1)
    pad_row = lax.broadcasted_iota(jnp.int32, (AUG_ROWS - D_HEAD, chunk), 0)
    for h in range(N_HEADS):
        lanes = slice(h * D_HEAD, (h + 1) * D_HEAD)
        one = slice(h, h + 1)
        w_intra = jnp.exp(jnp.where(src <= tgt, c_col[:, one] - big_m[one, :], -jnp.inf))
        s_t = s_raw[h] * w_intra
        inter = inter_all[h]
        wi = w_inter[one, :]
        num = _dot(v_t[h].astype(BF16), s_t.astype(BF16)) + wi * inter[0:D_HEAD, :]
        den = jnp.sum(s_t, axis=0, keepdims=True) + wi * inter[D_HEAD:D_HEAD + 1, :]
        hh = num * (1.0 / jnp.maximum(jnp.abs(den), exp_neg_m[one, :]))
        hh = hh * lax.rsqrt(jnp.mean(hh * hh, axis=0, keepdims=True) + EPS)
        hm_ref[:, lanes] = (_sigmoid(o_ref[:, lanes]) * (hh.T * gout_ref[:, lanes])).astype(BF16)
        ws = w_state[one, :]
        v_aug = jnp.concatenate(
            [v_t[h] * ws, jnp.where(pad_row == 0, jnp.broadcast_to(ws, pad_row.shape), 0.0)], axis=0)
        c_s[h] = decay[one, :] * c_s[h] + _dot(v_aug.astype(BF16), kb[h])
    m_s[...] = last(m)
    mrow_s[0:1, :] = m_col[chunk - 1:chunk, :]

    @pl.when(step == n_chunks - 1)
    def _():
        for h in range(N_HEADS):
            c_out_ref[0, h] = c_s[h, 0:D_HEAD, :]
            n_out_ref[0, h:h + 1, :] = c_s[h, D_HEAD:D_HEAD + 1, :]
        m_out_ref[0] = mrow_s[0:1, :]


def _mlstm_chunked(z, bg, gout, l, state, *, n_seq, seq_len, chunk, row0, name):
    n_chunks = seq_len // chunk
    gate_blk = z.shape[1] // GATE_LANES - 1
    blk0 = row0 // chunk
    zspec = lambda width, colblk: pl.BlockSpec((chunk, width), lambda b, c: (blk0 + b * n_chunks + c, colblk))
    c0, n0, m0 = state
    if chunk % GATE_LANES == 0:
        body = _mlstm_wide_kernel
        scratch = [pltpu.VMEM((N_HEADS, AUG_ROWS, D_HEAD), F32), pltpu.VMEM((SUBLANES, GATE_LANES), F32),
                   pltpu.VMEM((SUBLANES, GATE_LANES), F32)]
    else:
        body = _mlstm_chunk_kernel
        scratch = [pltpu.VMEM((N_HEADS, D_HEAD, D_HEAD), F32), pltpu.VMEM((SUBLANES, D_HEAD), F32),
                   pltpu.VMEM((SUBLANES, GATE_LANES), F32)]
    return pl.pallas_call(
        functools.partial(body, chunk=chunk, n_chunks=n_chunks),
        grid=(n_seq, n_chunks),
        in_specs=[zspec(512, COL_Q // 512), zspec(512, COL_K // 512), zspec(512, COL_V // 512),
                  zspec(512, COL_O // 512), zspec(GATE_LANES, gate_blk),
                  _layer(bg, l), _layer(gout, l),
                  pl.BlockSpec((1, N_HEADS, D_HEAD, D_HEAD), lambda b, c: (0, 0, 0, 0)),
                  pl.BlockSpec((1, N_HEADS, D_HEAD), lambda b, c: (0, 0, 0)),
                  pl.BlockSpec((1, 1, GATE_LANES), lambda b, c: (0, 0, 0))],
        out_specs=[pl.BlockSpec((chunk, D_MLSTM), lambda b, c: (b * n_chunks + c, 0)),
                   pl.BlockSpec((1, N_HEADS, D_HEAD, D_HEAD), lambda b, c: (b, 0, 0, 0)),
                   pl.BlockSpec((1, N_HEADS, D_HEAD), lambda b, c: (b, 0, 0)),
                   pl.BlockSpec((1, 1, GATE_LANES), lambda b, c: (b, 0, 0))],
        out_shape=[jax.ShapeDtypeStruct((n_seq * seq_len, D_MLSTM), BF16),
                   jax.ShapeDtypeStruct((n_seq, N_HEADS, D_HEAD, D_HEAD), F32),
                   jax.ShapeDtypeStruct((n_seq, N_HEADS, D_HEAD), F32),
                   jax.ShapeDtypeStruct((n_seq, 1, GATE_LANES), F32)],
        scratch_shapes=scratch,
        compiler_params=_params(("arbitrary", "arbitrary")),
        name=name,
    )(z, z, z, z, z, bg, gout, c0, n0, m0)


def _mlstm_step_kernel(*refs, n_seq, n_tok, layer, aliased):
    nh = N_HEADS
    q_refs, k_refs, v_refs, o_refs = refs[0:nh], refs[nh:2 * nh], refs[2 * nh:3 * nh], refs[3 * nh:4 * nh]
    g_ref, bg_ref, gout_ref, c0_ref, n0_ref, m0_ref = refs[4 * nh:4 * nh + 6]
    hm_ref, c_out_ref, n_out_ref, m_out_ref, inter_s, wv_s, hm_s = refs[4 * nh + 6 + int(aliased):]
    if not aliased:
        for other in range(c_out_ref.shape[0]):
            if other != layer:
                c_out_ref[other] = jnp.zeros(c_out_ref.shape[1:], F32)
        c_out_ref = c_out_ref.at[layer]

    def token(ref, t):
        return ref[pl.ds(t, n_seq, stride=n_tok), :]

    m0 = m0_ref[...]
    b, c, big_m = [], [], []
    for t in range(n_tok):
        ig, logf = _gate_lanes(token(g_ref, t) + bg_ref[...])
        b.append(logf if t == 0 else b[-1] + logf)
        c.append(ig - b[-1])
        big_m.append(jnp.maximum(c[-1], m0 if t == 0 else big_m[-1]))
    m_last = big_m[-1]
    decay = jnp.exp(m0 - m_last)
    scale = D_HEAD ** -0.5

    for j in range(n_seq):
        rows = slice(j * n_tok, (j + 1) * n_tok)
        for h in range(nh):
            inter_s[h, rows, :] = lax.dot_general(
                q_refs[h][rows, :].astype(BF16), c0_ref[j, h].astype(BF16), NT_DIMS, preferred_element_type=F32)

    for h in range(nh):
        lanes = slice(h * D_HEAD, (h + 1) * D_HEAD)
        one = slice(h, h + 1)
        q = [token(q_refs[h], t) for t in range(n_tok)]
        k = [token(k_refs[h], t) * scale for t in range(n_tok)]
        v = [token(v_refs[h], t) for t in range(n_tok)]
        n0 = n0_ref[:, lanes]
        for t in range(n_tok):
            mt = big_m[t][:, one]
            wi = jnp.exp(m0[:, one] - mt)
            num = wi * inter_s[h, pl.ds(t, n_seq, stride=n_tok), :]
            den = wi * jnp.sum(q[t] * n0, axis=1, keepdims=True)
            for s in range(t + 1):
                w = jnp.sum(q[t] * k[s], axis=1, keepdims=True) * jnp.exp(c[s][:, one] - mt)
                num = num + w * v[s]
                den = den + w
            hh = num / jnp.maximum(jnp.abs(den), jnp.exp(-(b[t][:, one] + mt)))
            hm_s[h, pl.ds(t, n_seq, stride=n_tok), :] = _head_out(hh, gout_ref[:, lanes], token(o_refs[h], t))
        n_new = decay[:, one] * n0
        for s in range(n_tok):
            ws = jnp.exp(c[s][:, one] - m_last[:, one])
            wv_s[h, pl.ds(s, n_seq, stride=n_tok), :] = ws * v[s]
            n_new = n_new + ws * k[s]
        n_out_ref[:, lanes] = n_new
        hm_ref[:, lanes] = hm_s[h].astype(BF16)
    m_out_ref[...] = b[-1] + m_last

    for j in range(n_seq):
        rows = slice(j * n_tok, (j + 1) * n_tok)
        for h in range(nh):
            kb = (k_refs[h][rows, :] * scale).astype(BF16)
            c_out_ref[j, h] = decay[j:j + 1, h:h + 1] * c0_ref[j, h] + lax.dot_general(
                wv_s[h, rows, :].astype(BF16), kb, TN_DIMS, preferred_element_type=F32)


def _mlstm_step(z, bg, gout, l, c_all, n0, m0, c_out_prev, *, n_seq, n_tok):
    sb = SAMPLE_SEQ_BLOCK
    rows = sb * n_tok
    head_specs = lambda col0: [pl.BlockSpec((rows, D_HEAD), functools.partial(
        lambda i, cb: (i, cb), cb=col0 // D_HEAD + h)) for h in range(N_HEADS)]
    c_spec = pl.BlockSpec((None, sb, N_HEADS, D_HEAD, D_HEAD), lambda i: (l, i, 0, 0, 0))
    n_z = 4 * N_HEADS + 1
    args = [z] * n_z + [bg, gout, c_all, n0, m0]
    specs = head_specs(COL_Q) + head_specs(COL_K) + head_specs(COL_V) + head_specs(COL_O) + [
        pl.BlockSpec((rows, GATE_LANES), lambda i: (i, COL_G // GATE_LANES)),
        _layer(bg, l), _layer(gout, l), c_spec,
        pl.BlockSpec((sb, D_MLSTM), lambda i: (i, 0)),
        pl.BlockSpec((sb, GATE_LANES), lambda i: (i, 0))]
    aliases = {}
    c_out_spec = c_spec
    if c_out_prev is not None:
        args.append(c_out_prev)
        specs.append(pl.BlockSpec(memory_space=pl.ANY))
        aliases = {len(args) - 1: 1}
    else:
        c_out_spec = pl.BlockSpec((c_all.shape[0], sb, N_HEADS, D_HEAD, D_HEAD), lambda i: (0, i, 0, 0, 0))
    return pl.pallas_call(
        functools.partial(_mlstm_step_kernel, n_seq=sb, n_tok=n_tok, layer=l, aliased=c_out_prev is not None),
        grid=(n_seq // sb,),
        in_specs=specs,
        out_specs=[pl.BlockSpec((rows, D_MLSTM), lambda i: (i, 0)), c_out_spec,
                   pl.BlockSpec((sb, D_MLSTM), lambda i: (i, 0)),
                   pl.BlockSpec((sb, GATE_LANES), lambda i: (i, 0))],
        out_shape=[jax.ShapeDtypeStruct((n_seq * n_tok, D_MLSTM), BF16),
                   jax.ShapeDtypeStruct(c_all.shape, F32),
                   jax.ShapeDtypeStruct(n0.shape, F32),
                   jax.ShapeDtypeStruct(m0.shape, F32)],
        scratch_shapes=[pltpu.VMEM((N_HEADS, rows, D_HEAD), F32)] * 3,
        input_output_aliases=aliases,
        compiler_params=_params(("arbitrary",)),
        name="mlstm_step",
    )(*args)


def _expm1(y):
    return jnp.tanh(0.5 * y) * (jnp.exp(y) + 1.0)


def _lru_gates(xc, wa_ref, ba_ref, wx_ref, bx_ref, lam_ref):
    xb = xc.astype(BF16)
    r = _sigmoid(_dot(xb, wa_ref[...]) + ba_ref[...])
    i = _sigmoid(_dot(xb, wx_ref[...]) + bx_ref[...])
    log_a = -LRU_C * r * jax.nn.softplus(-lam_ref[...])
    return jnp.exp(log_a), jnp.sqrt(-_expm1(2.0 * log_a)) * (i * xc)


def _lru_out(h, g, yg):
    h = h * lax.rsqrt(jnp.mean(h * h, axis=-1, keepdims=True) + EPS) * g
    return h * jax.nn.gelu(yg)


def _lru_rows(xr, yg, prev, carry, lru_refs, h_ref):
    cw_ref, cb_ref, wa_ref, ba_ref, wx_ref, bx_ref, lam_ref, gl_ref = lru_refs
    rows = xr.shape[0]
    ext = jnp.concatenate([prev, xr], axis=0)
    acc = ext * cw_ref[0:1, :]
    for j in range(1, CONV_WIDTH):
        acc = pltpu.roll(acc, 1, 0) + ext * cw_ref[j:j + 1, :]
    xc = acc[SUBLANES:, :] + cb_ref[...]
    a, bx = _lru_gates(xc, wa_ref, ba_ref, wx_ref, bx_ref, lam_ref)

    groups = rows // SUBLANES
    a3 = a.reshape(groups, SUBLANES, D_LRU)
    b3 = bx.reshape(groups, SUBLANES, D_LRU)
    sub = lax.broadcasted_iota(jnp.int32, a3.shape, 1)
    d = 1
    while d < SUBLANES:
        live = sub >= d
        b3 = a3 * jnp.where(live, pltpu.roll(b3, d, 1), 0.0) + b3
        a3 = a3 * jnp.where(live, pltpu.roll(a3, d, 1), 1.0)
        d *= 2
    for g in range(groups):
        hg = b3[g] + a3[g] * carry
        h_ref[g * SUBLANES:(g + 1) * SUBLANES, :] = hg
        carry = hg[SUBLANES - 1:SUBLANES, :]
    return _lru_out(h_ref[0:rows, :], gl_ref[...], yg), ext[rows:, :], carry


INPROJ_TILE = 512
LRU_ROWS = 128
D_ZMAIN = COL_XR + GATE_LANES


def _inproj_lru_kernel(x_ref, xt_ref, g_ref, w_ref, cw_ref, cb_ref, wa_ref, ba_ref, wx_ref, bx_ref, lam_ref, gl_ref,
                       z_ref, yl_ref, zt_ref, ylm_ref, h_out_ref, buf_out_ref,
                       xn_s, wl_s, wg_s, lr_s, h_s, prev_s, carry_s, mprev_s, mcarry_s,
                       *, n_dec, tiles_per_seq):
    step = pl.program_id(0)
    taps = CONV_WIDTH - 1
    lru_refs = (cw_ref, cb_ref, wa_ref, ba_ref, wx_ref, bx_ref, lam_ref, gl_ref)
    n_gate = 2 * N_HEADS
    w_main = lambda c0: w_ref[:, c0:c0 + 512]

    @pl.when(step == 0)
    def _():
        wl_s[...] = w_ref[:, COL_XR + n_gate:]
        lane = lax.broadcasted_iota(jnp.int32, (D_MODEL, GATE_LANES), 1)
        wg_s[...] = jnp.where(lane < n_gate, w_ref[:, COL_XR:COL_XR + GATE_LANES], 0.0).astype(BF16)

        rows = xt_ref.shape[0]
        xn = xn_s.at[0:rows]
        xn[...] = _rmsnorm(xt_ref[...], g_ref[...]).astype(BF16)
        for c0 in range(0, COL_XR, 512):
            zt_ref[:, c0:c0 + 512] = _dot(xn[...], w_main(c0))
        for c0 in range(0, COL_G - COL_XR, 512):
            zt_ref[:, COL_XR + c0:COL_XR + c0 + 512] = _dot(xn[...], wl_s[:, c0:c0 + 512])
        zt_ref[:, COL_G:] = _dot(xn[...], wg_s[...])

        y, prev, carry = _lru_rows(zt_ref[n_dec:, COL_XR:COL_YG], zt_ref[n_dec:, COL_YG:COL_G],
                                   jnp.zeros((SUBLANES, D_LRU), F32), jnp.zeros((1, D_LRU), F32), lru_refs, h_s)
        ylm_ref[...] = y.astype(BF16)
        mprev_s[...] = prev
        mcarry_s[0:1, :] = carry

    @pl.when(step > 0)
    def _():
        tile_in_seq = (step - 1) % tiles_per_seq

        @pl.when(tile_in_seq == 0)
        def _():
            prev_s[...] = mprev_s[...]
            carry_s[0:1, :] = mcarry_s[0:1, :]

        xn = xn_s.at[0:INPROJ_TILE]
        xn[...] = _rmsnorm(x_ref[...], g_ref[...]).astype(BF16)
        for c0 in range(0, COL_G - COL_XR, 512):
            lr_s[:, c0:c0 + 512] = _dot(xn[...], wl_s[:, c0:c0 + 512])
        prev, carry = prev_s[...], carry_s[0:1, :]
        z_cols = list(range(0, COL_XR, 512))
        for n, r0 in enumerate(range(0, INPROJ_TILE, LRU_ROWS)):
            rows = slice(r0, r0 + LRU_ROWS)
            for c0 in z_cols[n::INPROJ_TILE // LRU_ROWS]:
                z_ref[:, c0:c0 + 512] = _dot(xn[...], w_main(c0))
            y, prev, carry = _lru_rows(lr_s[rows, 0:D_LRU], lr_s[rows, D_LRU:], prev, carry, lru_refs, h_s)
            yl_ref[rows, :] = y.astype(BF16)
        prev_s[...] = prev
        carry_s[0:1, :] = carry
        z_ref[:, COL_XR:] = _dot(xn[...], wg_s[...])

        @pl.when(tile_in_seq == tiles_per_seq - 1)
        def _():
            h_out_ref[0] = carry_s[0:1, :]
            buf_out_ref[0] = prev_s[SUBLANES - taps:SUBLANES, :]


def _inproj_lru(x, g, w, lw, l, *, seq, n_dec):
    x_main, x_tail = x
    n_main = x_main.shape[0] // INPROJ_TILE
    n_tail = x_tail.shape[0]
    tiles_per_seq = seq // INPROJ_TILE
    n_seq = x_main.shape[0] // seq
    taps = CONV_WIDTH - 1
    main_idx = lambda i: (jnp.maximum(i - 1, 0), 0)
    seq_idx = lambda i: (jnp.maximum(i - 1, 0) // tiles_per_seq, 0, 0)
    fixed = lambda i: (0, 0)
    consts = [g, w, *lw]
    return pl.pallas_call(
        functools.partial(_inproj_lru_kernel, n_dec=n_dec, tiles_per_seq=tiles_per_seq),
        grid=(n_main + 1,),
        in_specs=[pl.BlockSpec((INPROJ_TILE, D_MODEL), main_idx),
                  pl.BlockSpec((n_tail, D_MODEL), fixed, pipeline_mode=pl.Buffered(1))]
        + [_layer(c, l) for c in consts],
        out_specs=[pl.BlockSpec((INPROJ_TILE, D_ZMAIN), main_idx),
                   pl.BlockSpec((INPROJ_TILE, D_LRU), main_idx),
                   pl.BlockSpec((n_tail, D_INP), fixed),
                   pl.BlockSpec((n_tail - n_dec, D_LRU), fixed),
                   pl.BlockSpec((1, 1, D_LRU), seq_idx),
                   pl.BlockSpec((1, taps, D_LRU), seq_idx)],
        out_shape=[jax.ShapeDtypeStruct((n_main * INPROJ_TILE, D_ZMAIN), F32),
                   jax.ShapeDtypeStruct((n_main * INPROJ_TILE, D_LRU), BF16),
                   jax.ShapeDtypeStruct((n_tail, D_INP), F32),
                   jax.ShapeDtypeStruct((n_tail - n_dec, D_LRU), BF16),
                   jax.ShapeDtypeStruct((n_seq, 1, D_LRU), F32),
                   jax.ShapeDtypeStruct((n_seq, taps, D_LRU), F32)],
        scratch_shapes=[pltpu.VMEM((max(n_tail, INPROJ_TILE), D_MODEL), BF16),
                        pltpu.VMEM((D_MODEL, 2 * D_LRU), BF16), pltpu.VMEM((D_MODEL, GATE_LANES), BF16),
                        pltpu.VMEM((INPROJ_TILE, 2 * D_LRU), F32), pltpu.VMEM((LRU_ROWS, D_LRU), F32),
                        pltpu.VMEM((SUBLANES, D_LRU), F32), pltpu.VMEM((SUBLANES, D_LRU), F32),
                        pltpu.VMEM((SUBLANES, D_LRU), F32), pltpu.VMEM((SUBLANES, D_LRU), F32)],
        compiler_params=_params(("arbitrary",)),
        name="inproj_lru",
    )(x_main, x_tail, *consts)


def _lru_step_kernel(*refs, n_seq, n_tok):
    nb = D_LRU // 128
    xr_refs, yg_refs = refs[0:nb], refs[nb:2 * nb]
    (cw_ref, cb_ref, wa_ref, ba_ref, wx_ref, bx_ref, lam_ref, gl_ref, h0_ref, buf0_ref,
     yl_ref, h_out_ref, buf_out_ref, xc_s, yg_s, yl_s) = refs[2 * nb:]
    taps = CONV_WIDTH - 1
    for cblk in range(nb):
        lanes = slice(cblk * 128, (cblk + 1) * 128)
        xp = [buf0_ref[:, j * D_LRU + cblk * 128:j * D_LRU + (cblk + 1) * 128] for j in range(taps)]
        xp += [xr_refs[cblk][pl.ds(t, n_seq, stride=n_tok), :] for t in range(n_tok)]
        for t in range(n_tok):
            xc = cb_ref[:, lanes]
            for j in range(CONV_WIDTH):
                xc = xc + xp[t + j] * cw_ref[j:j + 1, lanes]
            xc_s[t * n_seq:(t + 1) * n_seq, lanes] = xc
            yg_s[t * n_seq:(t + 1) * n_seq, lanes] = yg_refs[cblk][pl.ds(t, n_seq, stride=n_tok), :]
        for j in range(taps):
            buf_out_ref[:, j * D_LRU + cblk * 128:j * D_LRU + (cblk + 1) * 128] = xp[n_tok + j]
    a, bx = _lru_gates(xc_s[...], wa_ref, ba_ref, wx_ref, bx_ref, lam_ref)
    h = h0_ref[...]
    for t in range(n_tok):
        rows = slice(t * n_seq, (t + 1) * n_seq)
        h = a[rows] * h + bx[rows]
        y = _lru_out(h, gl_ref[...], yg_s[rows, :])
        for cblk in range(nb):
            yl_s[cblk, pl.ds(t, n_seq, stride=n_tok), :] = y[:, cblk * 128:(cblk + 1) * 128]
    for cblk in range(nb):
        yl_ref[:, cblk * 128:(cblk + 1) * 128] = yl_s[cblk].astype(BF16)
    h_out_ref[...] = h


def _lru_step(z, lw, l, state, *, n_seq, n_tok):
    rows = n_seq * n_tok
    nb = D_LRU // 128
    h0, buf0 = state
    whole = lambda a: pl.BlockSpec(a.shape, lambda i: (0,) * a.ndim)
    lane_specs = lambda col0: [pl.BlockSpec((rows, 128), functools.partial(
        lambda i, cb: (0, cb), cb=col0 // 128 + c)) for c in range(nb)]
    return pl.pallas_call(
        functools.partial(_lru_step_kernel, n_seq=n_seq, n_tok=n_tok),
        grid=(1,),
        in_specs=lane_specs(COL_XR) + lane_specs(COL_YG) + [_layer(w, l) for w in lw] + [whole(h0), whole(buf0)],
        out_specs=[pl.BlockSpec((rows, D_LRU), lambda i: (0, 0)), whole(h0), whole(buf0)],
        out_shape=[jax.ShapeDtypeStruct((rows, D_LRU), BF16),
                   jax.ShapeDtypeStruct(h0.shape, F32),
                   jax.ShapeDtypeStruct(buf0.shape, F32)],
        scratch_shapes=[pltpu.VMEM((rows, D_LRU), F32), pltpu.VMEM((rows, D_LRU), F32),
                        pltpu.VMEM((nb, rows, 128), F32)],
        compiler_params=_params(("arbitrary",)),
        name="lru_step",
    )(*([z] * (2 * nb)), *lw, h0, buf0)


def _block_diag(w):
    depth, n, d, _ = w.shape
    eye = jnp.eye(n, dtype=w.dtype)
    return (eye[None, :, None, :, None] * w[:, :, :, None, :]).reshape(depth, n * d, n * d)


def kernel(x_prompt, x_sample, state_mlstm_C, state_mlstm_n, state_mlstm_m, state_lru_h, state_conv, meta_tokens, g_ff1, w_ff1_gate, w_ff1_up, w_ff1_down, g_mix, w_in, b_gates, conv_w, conv_b, w_rg_a, b_rg_a, w_rg_x, b_rg_x, lru_lambda, g_mlstm_out, g_lru_out, w_out, g_ff2, w_ff2_gate, w_ff2_up, w_ff2_down, g_final):
    batch, seq, _ = x_prompt.shape
    n_samp, n_tok, _ = x_sample.shape
    depth = w_in.shape[0]
    n_prompt = batch * seq
    n_dec = n_samp * n_tok
    assert n_prompt % MAIN_TILE == 0 and seq % PROMPT_CHUNK == 0
    assert n_dec % N_META == 0 and n_samp % SAMPLE_SEQ_BLOCK == 0
    taps = CONV_WIDTH - 1

    x = (x_prompt.reshape(n_prompt, D_MODEL),
         jnp.concatenate([x_sample.reshape(n_dec, D_MODEL), meta_tokens.astype(F32)], axis=0))

    vec = lambda a: a.astype(F32).reshape(depth, 1, -1)
    ff1 = (vec(g_ff1), w_ff1_gate.astype(BF16), w_ff1_up.astype(BF16), w_ff1_down.astype(BF16))
    ff2 = (vec(g_ff2), w_ff2_gate.astype(BF16), w_ff2_up.astype(BF16), w_ff2_down.astype(BF16))
    n_gate = 2 * N_HEADS
    w_in_b = w_in.astype(BF16)
    bg = jnp.pad(b_gates.astype(F32), ((0, 0), (0, GATE_LANES - n_gate))).reshape(depth, 1, GATE_LANES)
    gout = vec(g_mlstm_out)
    wo = w_out.astype(BF16)
    lw = (conv_w.astype(F32), vec(conv_b), _block_diag(w_rg_a).astype(BF16), vec(b_rg_a),
          _block_diag(w_rg_x).astype(BF16), vec(b_rg_x), vec(lru_lambda), vec(g_lru_out))
    g_fin = g_final.astype(F32).reshape(1, D_MODEL)

    zero_state = (jnp.zeros((1, N_HEADS, D_HEAD, D_HEAD), F32), jnp.zeros((1, N_HEADS, D_HEAD), F32),
                  jnp.zeros((1, 1, GATE_LANES), F32))
    c_all = state_mlstm_C.astype(F32)
    n_all = state_mlstm_n.astype(F32).reshape(depth, n_samp, D_MLSTM)
    m_all = jnp.pad(state_mlstm_m.astype(F32), ((0, 0), (0, 0), (0, GATE_LANES - N_HEADS)))
    h_all = state_lru_h.astype(F32)
    buf_all = state_conv.astype(F32).reshape(depth, n_samp, taps * D_LRU)

    outs = {k: [] for k in ("pC", "pn", "pm", "ph", "pb", "sn", "sm", "sh", "sb")}
    s_c = None
    for l in range(depth):
        x = _ffn(x, *ff1, l)
        z_main, yl_main, z_tail, yl_meta, p_h, p_b = _inproj_lru(x, vec(g_mix), w_in_b, lw, l, seq=seq, n_dec=n_dec)

        hm_meta, m_c, m_n, m_m = _mlstm_chunked(z_tail, bg, gout, l, zero_state, n_seq=1, seq_len=N_META,
                                                chunk=N_META, row0=n_dec, name="mlstm_meta")
        hm_main, p_c, p_n, p_m = _mlstm_chunked(z_main, bg, gout, l, (m_c, m_n, m_m), n_seq=batch, seq_len=seq,
                                                chunk=PROMPT_CHUNK, row0=0, name="mlstm_prompt")
        hm_dec, s_c, s_n, s_m = _mlstm_step(z_tail, bg, gout, l, c_all, n_all[l], m_all[l], s_c,
                                            n_seq=n_samp, n_tok=n_tok)

        yl_dec, s_h, s_b = _lru_step(z_tail, lw, l, (h_all[l], buf_all[l]), n_seq=n_samp, n_tok=n_tok)

        mix = ((hm_main, jnp.concatenate([hm_dec, hm_meta], axis=0)),
               (yl_main, jnp.concatenate([yl_dec, yl_meta], axis=0)), wo)
        x = _ffn(x, *ff2, l, mix=mix, g_final=g_fin if l == depth - 1 else None)

        for key, val in (("pC", p_c), ("pn", p_n), ("pm", p_m[:, 0, :N_HEADS]), ("ph", p_h[:, 0]), ("pb", p_b),
                         ("sn", s_n.reshape(n_samp, N_HEADS, D_HEAD)), ("sm", s_m[:, :N_HEADS]),
                         ("sh", s_h), ("sb", s_b.reshape(n_samp, taps, D_LRU))):
            outs[key].append(val)

    y_main, y_tail = x
    st = {k: jnp.stack(v) for k, v in outs.items()}
    return (y_main.reshape(batch, seq, D_MODEL), y_tail[:n_dec].reshape(n_samp, n_tok, D_MODEL),
            st["pC"], st["pn"], st["pm"], st["ph"], st["pb"], s_c, st["sn"], st["sm"], st["sh"], st["sb"])
```
